```python
import jax, jax.numpy as jnp
from jax import lax
import numpy as np

D_MODEL = 1024
BATCH = 8
SEQ = 2048
DEPTH = 4
DEC_BATCH = 32
DEC_SEQ = 4
PAST_LEN = 8192
PAGE_SIZE = 128

N_META = 16
N_A = DEPTH // 2
N_B = DEPTH - N_A
POOL_WINDOWS = (2, 4, 8, 16)
N_POOL_GROUPS = 4
POOL_GROUP_DIM = D_MODEL // N_POOL_GROUPS
POOL_STATE = max(POOL_WINDOWS) - 1
N_HEADS = 16
HEAD_DIM = D_MODEL // N_HEADS
ATTN_SCALE = HEAD_DIM ** -0.5
Q_BLOCK = 128
N_GROUPS = 4
EXPERTS_PER_GROUP = 4
N_EXPERTS = N_GROUPS * EXPERTS_PER_GROUP
TOP_K = 2
D_EXPERT = D_MODEL // 2
FORGET_BIAS = 2.0
EPS = 1e-6
NEG_INF = -1e30

kernel_name = "yoco_pool_fox_hmoe_step"


def rms_norm(x, g):
    xf = x.astype(jnp.float32)
    y = xf * lax.rsqrt(jnp.mean(xf * xf, axis=-1, keepdims=True) + EPS)
    return (y * g.astype(jnp.float32)).astype(x.dtype)


def pool_mixer(u, prev, pos0, w_grp, scale):
    n, t, _ = u.shape
    ext = jnp.concatenate([prev.astype(u.dtype), u], axis=1)
    extf = ext.astype(jnp.float32)
    csum = jnp.concatenate([jnp.zeros_like(extf[:, :1]), jnp.cumsum(extf, axis=1)], axis=1)
    pos = pos0 + jnp.arange(t)
    end = csum[:, POOL_STATE + 1: POOL_STATE + 1 + t]
    groups = []
    for g, w in enumerate(POOL_WINDOWS):
        lo, hi = g * POOL_GROUP_DIM, (g + 1) * POOL_GROUP_DIM
        start = csum[:, POOL_STATE + 1 - w: POOL_STATE + 1 - w + t, lo:hi]
        cnt = jnp.minimum(pos + 1, w).astype(jnp.float32)[None, :, None]
        groups.append((end[..., lo:hi] - start) / cnt)
    pooled = jnp.stack(groups, axis=2)
    diff = (pooled - u.astype(jnp.float32).reshape(n, t, N_POOL_GROUPS, POOL_GROUP_DIM)).astype(u.dtype)
    out = jnp.einsum('ntgc,gcd->ntgd', diff, w_grp).reshape(n, t, D_MODEL)
    return out * scale, ext[:, -POOL_STATE:]


def hier_moe(x, norm_g, w_rg, w_re, w_gate, w_up, w_down):
    u = rms_norm(x, norm_g)
    lg = jnp.einsum('ntd,dg->ntg', u, w_rg).astype(jnp.float32)
    pg = jax.nn.softmax(lg, axis=-1)
    p_top, g_sel = lax.top_k(pg, 1)
    le_all = jnp.einsum('ntd,gde->ntge', u, w_re).astype(jnp.float32)
    le = jnp.take_along_axis(le_all, g_sel[..., None], axis=2)[:, :, 0, :]
    top_v, top_i = lax.top_k(le, TOP_K)
    gate = p_top * jax.nn.softmax(top_v, axis=-1)
    expert_id = g_sel * EXPERTS_PER_GROUP + top_i
    dense_w = jnp.sum(jax.nn.one_hot(expert_id, N_EXPERTS, dtype=jnp.float32) * gate[..., None], axis=-2)
    h = jax.nn.silu(jnp.einsum('ntd,edf->ntef', u, w_gate)) * jnp.einsum('ntd,edf->ntef', u, w_up)
    h = h * dense_w[..., None].astype(h.dtype)
    return jnp.einsum('ntef,efd->ntd', h, w_down)


def shared_kv(h, kv_norm_g, w_kvf, b_f, k_norm_g):
    n, t, _ = h.shape
    p = rms_norm(h, kv_norm_g) @ w_kvf
    k = rms_norm(p[..., :D_MODEL].reshape(n, t, N_HEADS, HEAD_DIM), k_norm_g)
    v = p[..., D_MODEL:2 * D_MODEL].reshape(n, t, N_HEADS, HEAD_DIM)
    logf = jax.nn.log_sigmoid((p[..., 2 * D_MODEL:] + b_f).astype(jnp.float32))
    return k, v, logf


def project_queries(h, norm_g, w_q, q_norm_g):
    n, t, _ = h.shape
    q = (rms_norm(h, norm_g) @ w_q).reshape(n, t, N_HEADS, HEAD_DIM)
    return rms_norm(q, q_norm_g)


def forget_attn_prompt(q, k, v, c):
    n, t, _, _ = q.shape
    tp = -(-t // Q_BLOCK) * Q_BLOCK
    pad = tp - t
    qp = jnp.pad(q, ((0, 0), (0, pad), (0, 0), (0, 0)))
    kp = jnp.pad(k, ((0, 0), (0, pad), (0, 0), (0, 0)))
    vp = jnp.pad(v, ((0, 0), (0, pad), (0, 0), (0, 0)))
    ct = jnp.pad(c, ((0, 0), (0, pad), (0, 0))).transpose(0, 2, 1)
    nb = tp // Q_BLOCK
    q_blocks = qp.reshape(n, nb, Q_BLOCK, N_HEADS, HEAD_DIM).transpose(1, 0, 2, 3, 4)
    c_blocks = ct.reshape(n, N_HEADS, nb, Q_BLOCK).transpose(2, 0, 1, 3)
    starts = jnp.arange(nb) * Q_BLOCK
    kpos = jnp.arange(tp)

    def one_block(args):
        qb, cb, s0 = args
        s = jnp.einsum('nqhd,nkhd->nhqk', qb, kp).astype(jnp.float32) * ATTN_SCALE
        s = s + cb[..., None] - ct[:, :, None, :]
        qpos = s0 + jnp.arange(Q_BLOCK)
        s = jnp.where(kpos[None, :] <= qpos[:, None], s, NEG_INF)
        p = jax.nn.softmax(s, axis=-1)
        return jnp.einsum('nhqk,nkhd->nqhd', p.astype(vp.dtype), vp)

    out = lax.map(one_block, (q_blocks, c_blocks, starts))
    return out.transpose(1, 0, 2, 3, 4).reshape(n, tp, N_HEADS, HEAD_DIM)[:, :t]


def forget_attn_sample(q, k_new, v_new, c_new, k_past, v_past, c_past):
    s_len = q.shape[1]
    past = k_past.shape[1]
    cq = c_new.transpose(0, 2, 1)[..., None]
    s_past = jnp.einsum('nqhd,nkhd->nhqk', q, k_past).astype(jnp.float32) * ATTN_SCALE
    s_past = s_past + cq - c_past.transpose(0, 2, 1)[:, :, None, :]
    s_new = jnp.einsum('nqhd,nkhd->nhqk', q, k_new).astype(jnp.float32) * ATTN_SCALE
    s_new = s_new + cq - c_new.transpose(0, 2, 1)[:, :, None, :]
    causal = jnp.arange(s_len)[None, :] <= jnp.arange(s_len)[:, None]
    s_new = jnp.where(causal, s_new, NEG_INF)
    p = jax.nn.softmax(jnp.concatenate([s_past, s_new], axis=-1), axis=-1)
    out = jnp.einsum('nhqk,nkhd->nqhd', p[..., :past].astype(v_past.dtype), v_past)
    return out + jnp.einsum('nhqk,nkhd->nqhd', p[..., past:].astype(v_new.dtype), v_new)


def setup_inputs(seed: int = 0) -> dict:
    key = jax.random.key(seed)
    ks = jax.random.split(key, 26)
    f32 = jnp.float32
    n_pages = PAST_LEN // PAGE_SIZE
    n_used = DEC_BATCH * n_pages
    n_phys = n_used + max(1, n_used // 4)

    def nrm(k, shape, scale=1.0):
        return jax.random.normal(k, shape, f32) * scale

    def gain(k, shape):
        return 1.0 + 0.1 * nrm(k, shape)

    return {
        'x_prompt': nrm(ks[0], (BATCH, SEQ, D_MODEL)),
        'x_sample': nrm(ks[1], (DEC_BATCH, DEC_SEQ, D_MODEL)),
        'state_pool': nrm(ks[2], (N_A, DEC_BATCH, POOL_STATE, D_MODEL)),
        'cache_k': nrm(ks[3], (n_phys, PAGE_SIZE, N_HEADS, HEAD_DIM)),
        'cache_v': nrm(ks[4], (n_phys, PAGE_SIZE, N_HEADS, HEAD_DIM)),
        'cache_logf': jax.nn.log_sigmoid(FORGET_BIAS + nrm(ks[5], (n_phys, PAGE_SIZE, N_HEADS))),
        'page_table': jax.random.permutation(ks[6], n_phys)[:n_used].reshape(DEC_BATCH, n_pages).astype(jnp.int32),
        'meta': nrm(ks[7], (N_META, D_MODEL)),
        'pool_norm_g': gain(ks[8], (N_A, D_MODEL)),
        'pool_w': nrm(ks[9], (N_A, N_POOL_GROUPS, POOL_GROUP_DIM, POOL_GROUP_DIM), POOL_GROUP_DIM ** -0.5),
        'pool_scale': gain(ks[10], (N_A, D_MODEL)),
        'kv_norm_g': gain(ks[11], (D_MODEL,)),
        'w_kvf': nrm(ks[12], (D_MODEL, 2 * D_MODEL + N_HEADS), D_MODEL ** -0.5),
        'b_f': FORGET_BIAS + 0.5 * nrm(ks[13], (N_HEADS,)),
        'k_norm_g': gain(ks[14], (HEAD_DIM,)),
        'attn_norm_g': gain(ks[15], (N_B, D_MODEL)),
        'w_q': nrm(ks[16], (N_B, D_MODEL, D_MODEL), D_MODEL ** -0.5),
        'q_norm_g': gain(ks[17], (N_B, HEAD_DIM)),
        'w_o': nrm(ks[18], (N_B, D_MODEL, D_MODEL), D_MODEL ** -0.5),
        'ffn_norm_g': gain(ks[19], (DEPTH, D_MODEL)),
        'w_router_group': nrm(ks[20], (DEPTH, D_MODEL, N_GROUPS), D_MODEL ** -0.5),
        'w_router_expert': nrm(ks[21], (DEPTH, N_GROUPS, D_MODEL, EXPERTS_PER_GROUP), D_MODEL ** -0.5),
        'w_gate': nrm(ks[22], (DEPTH, N_EXPERTS, D_MODEL, D_EXPERT), D_MODEL ** -0.5),
        'w_up': nrm(ks[23], (DEPTH, N_EXPERTS, D_MODEL, D_EXPERT), D_MODEL ** -0.5),
        'w_down': nrm(ks[24], (DEPTH, N_EXPERTS, D_EXPERT, D_MODEL), D_EXPERT ** -0.5),
    }


def reference(x_prompt, x_sample, state_pool, cache_k, cache_v, cache_logf, page_table, meta,
              pool_norm_g, pool_w, pool_scale, kv_norm_g, w_kvf, b_f, k_norm_g,
              attn_norm_g, w_q, q_norm_g, w_o, ffn_norm_g, w_router_group, w_router_expert,
              w_gate, w_up, w_down):
    n_p = x_prompt.shape[0]
    n_s = x_sample.shape[0]
    past_len = page_table.shape[1] * cache_k.shape[1]
    meta_rows = jnp.broadcast_to(meta.astype(x_prompt.dtype)[None], (n_p, N_META, D_MODEL))
    hp = jnp.concatenate([meta_rows, x_prompt], axis=1)
    hs = x_sample
    pool_new_p, pool_new_s = [], []
    for layer in range(DEPTH):
        if layer < N_A:
            up = rms_norm(hp, pool_norm_g[layer])
            us = rms_norm(hs, pool_norm_g[layer])
            zero_prev = jnp.zeros((n_p, POOL_STATE, D_MODEL), up.dtype)
            mp, tail_p = pool_mixer(up, zero_prev, 0, pool_w[layer], pool_scale[layer])
            ms, tail_s = pool_mixer(us, state_pool[layer], past_len, pool_w[layer], pool_scale[layer])
            hp = hp + mp
            hs = hs + ms
            pool_new_p.append(tail_p)
            pool_new_s.append(tail_s)
        else:
            if layer == N_A:
                k_p, v_p, logf_p = shared_kv(hp, kv_norm_g, w_kvf, b_f, k_norm_g)
                k_s, v_s, logf_s = shared_kv(hs, kv_norm_g, w_kvf, b_f, k_norm_g)
                c_p = jnp.cumsum(logf_p, axis=1)
                k_past = cache_k[page_table].reshape(n_s, past_len, N_HEADS, HEAD_DIM)
                v_past = cache_v[page_table].reshape(n_s, past_len, N_HEADS, HEAD_DIM)
                logf_past = cache_logf[page_table].reshape(n_s, past_len, N_HEADS).astype(jnp.float32)
                c_all = jnp.cumsum(jnp.concatenate([logf_past, logf_s], axis=1), axis=1)
                c_past, c_s = c_all[:, :past_len], c_all[:, past_len:]
            j = layer - N_A
            qp = project_queries(hp, attn_norm_g[j], w_q[j], q_norm_g[j])
            qs = project_queries(hs, attn_norm_g[j], w_q[j], q_norm_g[j])
            op = forget_attn_prompt(qp, k_p, v_p, c_p)
            os_ = forget_attn_sample(qs, k_s, v_s, c_s, k_past, v_past, c_past)
            hp = hp + op.reshape(hp.shape) @ w_o[j]
            hs = hs + os_.reshape(hs.shape) @ w_o[j]
        hp = hp + hier_moe(hp, ffn_norm_g[layer], w_router_group[layer], w_router_expert[layer],
                           w_gate[layer], w_up[layer], w_down[layer])
        hs = hs + hier_moe(hs, ffn_norm_g[layer], w_router_group[layer], w_router_expert[layer],
                           w_gate[layer], w_up[layer], w_down[layer])
    y_prompt = hp[:, N_META:]
    y_sample = hs
    pool_state_prompt = jnp.stack(pool_new_p, axis=0)
    pool_state_sample = jnp.stack(pool_new_s, axis=0)
    return (y_prompt, y_sample, pool_state_prompt, pool_state_sample, k_p, v_p, logf_p, k_s, v_s, logf_s)
```

```python
import functools

import jax
import jax.numpy as jnp
from jax import lax
from jax.experimental import pallas as pl
from jax.experimental.pallas import tpu as pltpu

F32 = jnp.float32
BF16 = jnp.bfloat16
I32 = jnp.int32

N_META = 16
POOL_WINDOWS = (2, 4, 8, 16)
POOL_STATE = max(POOL_WINDOWS) - 1
N_HEADS = 16
HEAD_DIM = 64
N_GROUPS = 4
EXPERTS_PER_GROUP = 4
N_EXPERTS = N_GROUPS * EXPERTS_PER_GROUP
ATTN_SCALE = HEAD_DIM ** -0.5
EPS = 1e-6
NEG_INF = -1e30

LANES = 128
SUBLANES = 8
MXU_DIM = 256
VMEM_LIMIT_BYTES = 56 * 1024 * 1024


def _params(sem, vmem=None):
    return pltpu.CompilerParams(dimension_semantics=sem, vmem_limit_bytes=vmem)


def _rms(x, g):
    return x * lax.rsqrt(jnp.mean(x * x, axis=-1, keepdims=True) + EPS) * g


def _dot(a, b):
    return jnp.dot(a, b, preferred_element_type=F32)


def _dot_nt(a, b):
    return lax.dot_general(a, b, (((1,), (1,)), ((), ())), preferred_element_type=F32)


def _split2(a):
    hi = a.astype(BF16)
    lo = (a - hi.astype(F32)).astype(BF16)
    return hi, lo


def _split3(a):
    hi = a.astype(BF16)
    r = a - hi.astype(F32)
    mid = r.astype(BF16)
    lo = (r - mid.astype(F32)).astype(BF16)
    return hi, mid, lo


def _dot3(a, b):
    ah, al = _split2(a)
    bh, bl = _split2(b)
    return _dot(ah, bh) + (_dot(ah, bl) + _dot(al, bh))


def _dot_exact_rhs(mask_bf16, b):
    b0, b1, b2 = _split3(b)
    return _dot(mask_bf16, b0) + (_dot(mask_bf16, b1) + _dot(mask_bf16, b2))


def _head_norm(k, gg, gain):
    ksq = k * k
    hi, lo = _split2(ksq)
    parts = []
    for c in range(k.shape[1] // MXU_DIM):
        sl = slice(c * MXU_DIM, (c + 1) * MXU_DIM)
        parts.append(_dot(hi[:, sl], gg) + _dot(lo[:, sl], gg))
    ms = jnp.concatenate(parts, axis=1) * (1.0 / HEAD_DIM)
    return k * lax.rsqrt(ms + EPS) * gain


def _pool_main_kernel(h_ref, hm_ref, g_ref, w_ref, sc_ref, o_ref, tail_ref, ext_ref, *, tt, n_t):
    i = pl.program_id(1)
    g = g_ref[...]
    gd = h_ref.shape[1] // len(POOL_WINDOWS)

    @pl.when(i == 0)
    def _():
        ext_ref[0:N_META, :] = _rms(hm_ref[...], g)

    x = h_ref[...]
    u = _rms(x, g)
    ext_ref[N_META:N_META + tt, :] = u
    outs = []
    for gi, w in enumerate(POOL_WINDOWS):
        lo, hi = gi * gd, (gi + 1) * gd
        ug = u[:, lo:hi]
        acc = ug
        for k in range(1, w):
            acc = acc + ext_ref[N_META - k:N_META - k + tt, lo:hi]
        diff = acc * (1.0 / w) - ug
        outs.append(_dot3(diff, w_ref[gi]))
    o_ref[...] = x + jnp.concatenate(outs, axis=1) * sc_ref[...]
    ext_ref[0:N_META, :] = ext_ref[tt:tt + N_META, :]

    @pl.when(i == n_t - 1)
    def _():
        tail_ref[0] = u[tt - N_META:tt, :]


def _pool_main(h, h_meta, g, w, sc, *, n_batch, seq, tt):
    t_rows, d = h.shape
    n_t = seq // tt
    kern = functools.partial(_pool_main_kernel, tt=tt, n_t=n_t)
    return pl.pallas_call(
        kern,
        grid=(n_batch, n_t),
        in_specs=[
            pl.BlockSpec((tt, d), lambda b, i: (b * n_t + i, 0)),
            pl.BlockSpec((N_META, d), lambda b, i: (b, 0)),
            pl.BlockSpec((1, d), lambda b, i: (0, 0)),
            pl.BlockSpec(w.shape, lambda b, i: (0, 0, 0)),
            pl.BlockSpec((1, d), lambda b, i: (0, 0)),
        ],
        out_specs=[
            pl.BlockSpec((tt, d), lambda b, i: (b * n_t + i, 0)),
            pl.BlockSpec((1, N_META, d), lambda b, i: (b, 0, 0)),
        ],
        out_shape=[jax.ShapeDtypeStruct(h.shape, F32), jax.ShapeDtypeStruct((n_batch, N_META, d), F32)],
        scratch_shapes=[pltpu.VMEM((N_META + tt, d), F32)],
        input_output_aliases={0: 0},
        compiler_params=_params(("arbitrary", "arbitrary"), VMEM_LIMIT_BYTES),
        name="pool_main",
    )(h, h_meta, g, w, sc)


def _pool_small_kernel(x_ref, prev_ref, g_ref, w_ref, sc_ref, y_ref, u_ref, *, n_new, pos0):
    g = g_ref[...]
    n_seq, d = x_ref.shape[1], x_ref.shape[2]
    gd = d // len(POOL_WINDOWS)
    us = [_rms(x_ref[t], g) for t in range(n_new)]
    ext = [prev_ref[j] for j in range(POOL_STATE)] + us
    outs = []
    for gi, w in enumerate(POOL_WINDOWS):
        lo, hi = gi * gd, (gi + 1) * gd
        diffs = []
        for t in range(n_new):
            ug = us[t][:, lo:hi]
            acc = ug
            for k in range(1, w):
                acc = acc + ext[POOL_STATE + t - k][:, lo:hi]
            cnt = float(min(pos0 + t + 1, w))
            diffs.append(acc / cnt - ug)
        outs.append(_dot3(jnp.concatenate(diffs, axis=0), w_ref[gi]))
    out = jnp.concatenate(outs, axis=1) * sc_ref[...]
    for t in range(n_new):
        y_ref[t] = x_ref[t] + out[t * n_seq:(t + 1) * n_seq, :]
        u_ref[t] = us[t]


def _pool_small(x, prev, g, w, sc, *, pos0):
    n_new = x.shape[0]
    kern = functools.partial(_pool_small_kernel, n_new=n_new, pos0=pos0)
    return pl.pallas_call(
        kern,
        out_shape=[jax.ShapeDtypeStruct(x.shape, F32), jax.ShapeDtypeStruct(x.shape, F32)],
        compiler_params=_params(None, VMEM_LIMIT_BYTES),
        name="pool_small",
    )(x, prev, g, w, sc)


def _moe_kernel(h_ref, g_ref, wr_ref, wg_ref, wu_ref, wd_ref, o_ref,
                u2d, slots, x_t, o_t, rt_i, rt_f, cnt_v, rt_si, rt_sf, cnt_s, lst,
                *, blk, tm, pitch):
    e = pl.program_id(1)
    n_chunks = blk // LANES
    d = h_ref.shape[1]
    n_slab = d // LANES

    @pl.when(e == 0)
    def _router():
        g = g_ref[...]
        wr_hi, wr_lo = _split2(wr_ref[...])
        sub16 = lax.broadcasted_iota(I32, (N_EXPERTS, LANES), 0)
        tri = (lax.broadcasted_iota(I32, (LANES, LANES), 0)
               < lax.broadcasted_iota(I32, (LANES, LANES), 1)).astype(BF16)

        def chunk(c, carry):
            r0 = pl.multiple_of(c * LANES, LANES)
            u = _rms(h_ref[pl.ds(r0, LANES), :], g)
            for j in range(n_slab):
                u2d[pl.ds(c * (LANES * n_slab) + j, LANES, stride=n_slab), :] = u[:, j * LANES:(j + 1) * LANES]
            u_hi, u_lo = _split2(u)
            lt = _dot_nt(wr_hi, u_hi) + (_dot_nt(wr_hi, u_lo) + _dot_nt(wr_lo, u_hi))
            gl = [lt[k:k + 1, :] for k in range(N_GROUPS)]
            gmax = jnp.maximum(jnp.maximum(gl[0], gl[1]), jnp.maximum(gl[2], gl[3]))
            gsel = jnp.where(gl[0] >= gmax, 0, jnp.where(gl[1] >= gmax, 1, jnp.where(gl[2] >= gmax, 2, 3)))
            denom = (jnp.exp(gl[0] - gmax) + jnp.exp(gl[1] - gmax)) + (jnp.exp(gl[2] - gmax) + jnp.exp(gl[3] - gmax))
            p_top = 1.0 / denom
            le = []
            for k in range(EXPERTS_PER_GROUP):
                rows = [lt[N_GROUPS + gi * EXPERTS_PER_GROUP + k:N_GROUPS + gi * EXPERTS_PER_GROUP + k + 1, :]
                        for gi in range(N_GROUPS)]
                le.append(jnp.where(gsel == 0, rows[0], jnp.where(gsel == 1, rows[1],
                                                                   jnp.where(gsel == 2, rows[2], rows[3]))))
            v1 = jnp.maximum(jnp.maximum(le[0], le[1]), jnp.maximum(le[2], le[3]))
            i1 = jnp.where(le[0] >= v1, 0, jnp.where(le[1] >= v1, 1, jnp.where(le[2] >= v1, 2, 3)))
            rest = [jnp.where(i1 == k, NEG_INF, le[k]) for k in range(EXPERTS_PER_GROUP)]
            v2 = jnp.maximum(jnp.maximum(rest[0], rest[1]), jnp.maximum(rest[2], rest[3]))
            i2 = jnp.where((rest[0] >= v2) & (i1 != 0), 0,
                           jnp.where((rest[1] >= v2) & (i1 != 1), 1,
                                     jnp.where((rest[2] >= v2) & (i1 != 2), 2, 3)))
            ex = jnp.exp(v2 - v1)
            gate1 = p_top / (1.0 + ex)
            gate2 = p_top * ex / (1.0 + ex)
            e1 = gsel * EXPERTS_PER_GROUP + i1
            e2 = gsel * EXPERTS_PER_GROUP + i2
            oh1 = (sub16 == e1).astype(F32)
            oh2 = (sub16 == e2).astype(F32)
            both = oh1 + oh2
            rank = _dot(both.astype(BF16), tri) + carry
            pos1 = jnp.sum(oh1 * rank, axis=0, keepdims=True).astype(I32)
            pos2 = jnp.sum(oh2 * rank, axis=0, keepdims=True).astype(I32)
            rt_i[c] = jnp.zeros((SUBLANES, LANES), I32)
            rt_f[c] = jnp.zeros((SUBLANES, LANES), F32)
            rt_i[c, 0:1, :] = e1 * blk + pos1
            rt_i[c, 1:2, :] = e2 * blk + pos2
            rt_f[c, 0:1, :] = gate1
            rt_f[c, 1:2, :] = gate2
            return carry + jnp.sum(both, axis=1, keepdims=True)

        total = lax.fori_loop(0, n_chunks, chunk, jnp.zeros((N_EXPERTS, 1), F32))
        cnt_v[...] = jnp.broadcast_to(total, (N_EXPERTS, LANES)).astype(I32)
        pltpu.sync_copy(rt_i, rt_si)
        pltpu.sync_copy(rt_f, rt_sf)
        pltpu.sync_copy(cnt_v, cnt_s)

        def invert(c, _):
            for l in range(LANES):
                tok2 = (c * LANES + l) * 2
                lst[rt_si[c, 0, l]] = tok2
                lst[rt_si[c, 1, l]] = tok2 + 1
            return 0

        lax.fori_loop(0, n_chunks, invert, 0)

    n_e = cnt_s[e, 0]
    n_tiles = (n_e + (tm - 1)) // tm

    def tile(ti, _):
        base = e * blk + ti * tm
        rem = n_e - ti * tm
        for r in range(tm):
            tok = lst[base + jnp.minimum(r, rem - 1)] >> 1
            src = pl.multiple_of(tok * n_slab, n_slab)
            x_t[pl.ds(r, n_slab, stride=pitch), :] = u2d[pl.ds(src, n_slab), :]
        x = jnp.concatenate([x_t[j * pitch:j * pitch + tm, :] for j in range(n_slab)], axis=1).astype(BF16)
        hg = _dot(x, wg_ref[0])
        hu = _dot(x, wu_ref[0])
        act = (hg * jax.nn.sigmoid(hg) * hu).astype(BF16)
        o = _dot(act, wd_ref[0])
        for j in range(n_slab):
            o_t[j * pitch:j * pitch + tm, :] = o[:, j * LANES:(j + 1) * LANES]
        for r in range(tm):
            v = lst[base + jnp.minimum(r, rem - 1)]
            tok, k = v >> 1, v & 1
            gate = rt_sf[tok >> 7, k, tok & (LANES - 1)]
            dst = jnp.where(r < rem, k * blk + tok, 2 * blk)
            dst = pl.multiple_of(dst * n_slab, n_slab)
            slots[pl.ds(dst, n_slab), :] = o_t[pl.ds(r, n_slab, stride=pitch), :] * gate
        return 0

    lax.fori_loop(0, n_tiles, tile, 0)

    @pl.when(e == N_EXPERTS - 1)
    def _combine():
        def chunk(c, _):
            r0 = pl.multiple_of(c * LANES, LANES)
            for j in range(n_slab):
                a = slots[pl.ds(c * (LANES * n_slab) + j, LANES, stride=n_slab), :]
                b = slots[pl.ds((blk + c * LANES) * n_slab + j, LANES, stride=n_slab), :]
                o_ref[pl.ds(r0, LANES), j * LANES:(j + 1) * LANES] = (
                    h_ref[pl.ds(r0, LANES), j * LANES:(j + 1) * LANES] + (a + b))
            return 0

        lax.fori_loop(0, n_chunks, chunk, 0)


def _moe(h, g, wr_t, wg, wu, wd, *, blk, tm):
    t_rows, d = h.shape
    f = wg.shape[2]
    n_blocks = t_rows // blk
    n_chunks = blk // LANES
    n_slab = d // LANES
    pitch = tm + SUBLANES
    kern = functools.partial(_moe_kernel, blk=blk, tm=tm, pitch=pitch)
    return pl.pallas_call(
        kern,
        grid=(n_blocks, N_EXPERTS),
        in_specs=[
            pl.BlockSpec((blk, d), lambda i, e: (i, 0)),
            pl.BlockSpec((1, d), lambda i, e: (0, 0)),
            pl.BlockSpec(wr_t.shape, lambda i, e: (0, 0)),
            pl.BlockSpec((1, d, f), lambda i, e: (e, 0, 0)),
            pl.BlockSpec((1, d, f), lambda i, e: (e, 0, 0)),
            pl.BlockSpec((1, f, d), lambda i, e: (e, 0, 0)),
        ],
        out_specs=pl.BlockSpec((blk, d), lambda i, e: (i, 0)),
        out_shape=jax.ShapeDtypeStruct(h.shape, F32),
        scratch_shapes=[
            pltpu.VMEM((blk * n_slab, LANES), F32),
            pltpu.VMEM(((2 * blk + 1) * n_slab, LANES), F32),
            pltpu.VMEM((n_slab * pitch, LANES), F32),
            pltpu.VMEM((n_slab * pitch, LANES), F32),
            pltpu.VMEM((n_chunks, SUBLANES, LANES), I32),
            pltpu.VMEM((n_chunks, SUBLANES, LANES), F32),
            pltpu.VMEM((N_EXPERTS, LANES), I32),
            pltpu.SMEM((n_chunks, SUBLANES, LANES), I32),
            pltpu.SMEM((n_chunks, SUBLANES, LANES), F32),
            pltpu.SMEM((N_EXPERTS, LANES), I32),
            pltpu.SMEM((N_EXPERTS * blk,), I32),
        ],
        input_output_aliases={0: 0},
        compiler_params=_params(("arbitrary", "arbitrary"), VMEM_LIMIT_BYTES),
        name="moe",
    )(h, g, wr_t, wg, wu, wd)


def _kv_kernel(h_ref, g_ref, wkv_ref, wf_ref, bf_ref, kg_ref, gg_ref, k_ref, v_ref, kb_ref, vb_ref, lf_ref):
    d = h_ref.shape[1]
    u = _rms(h_ref[...], g_ref[...])
    p = _dot(u.astype(BF16), wkv_ref[...])
    kn = _head_norm(p[:, :d], gg_ref[...], kg_ref[...])
    v = p[:, d:]
    k_ref[...] = kn
    v_ref[...] = v
    kb_ref[...] = kn.astype(BF16)
    vb_ref[...] = v.astype(BF16)
    z = _dot3(u, wf_ref[...])[:, :N_HEADS] + bf_ref[...]
    lf_ref[...] = jnp.minimum(z, 0.0) - jnp.log1p(jnp.exp(-jnp.abs(z)))


def _kv_proj(h, g, wkv, wf, bfr, kg, gg, *, tt):
    t_rows, d = h.shape
    row = lambda i: (i, 0)
    const = lambda i: (0, 0)
    return pl.pallas_call(
        _kv_kernel,
        grid=(t_rows // tt,),
        in_specs=[
            pl.BlockSpec((tt, d), row),
            pl.BlockSpec((1, d), const),
            pl.BlockSpec(wkv.shape, const),
            pl.BlockSpec(wf.shape, const),
            pl.BlockSpec((1, N_HEADS), const),
            pl.BlockSpec((1, d), const),
            pl.BlockSpec(gg.shape, const),
        ],
        out_specs=[pl.BlockSpec((tt, d), row)] * 4 + [pl.BlockSpec((tt, N_HEADS), row)],
        out_shape=[jax.ShapeDtypeStruct((t_rows, d), F32), jax.ShapeDtypeStruct((t_rows, d), F32),
                   jax.ShapeDtypeStruct((t_rows, d), BF16), jax.ShapeDtypeStruct((t_rows, d), BF16),
                   jax.ShapeDtypeStruct((t_rows, N_HEADS), F32)],
        compiler_params=_params(("arbitrary",), VMEM_LIMIT_BYTES),
        name="kv_proj",
    )(h, g, wkv, wf, bfr, kg, gg)


def _q_kernel(h_ref, g_ref, wq_ref, qg_ref, gg_ref, q_ref):
    u = _rms(h_ref[...], g_ref[...])
    q = _dot(u.astype(BF16), wq_ref[...])
    q_ref[...] = (_head_norm(q, gg_ref[...], qg_ref[...]) * ATTN_SCALE).astype(BF16)


def _q_proj(h, g, wq, qg, gg, *, tt):
    t_rows, d = h.shape
    row = lambda i: (i, 0)
    const = lambda i: (0, 0)
    return pl.pallas_call(
        _q_kernel,
        grid=(t_rows // tt,),
        in_specs=[pl.BlockSpec((tt, d), row), pl.BlockSpec((1, d), const), pl.BlockSpec(wq.shape, const),
                  pl.BlockSpec((1, d), const), pl.BlockSpec(gg.shape, const)],
        out_specs=pl.BlockSpec((tt, d), row),
        out_shape=jax.ShapeDtypeStruct((t_rows, d), BF16),
        compiler_params=_params(("arbitrary",), VMEM_LIMIT_BYTES),
        name="q_proj",
    )(h, g, wq, qg, gg)


def _o_kernel(h_ref, a_ref, wo_ref, o_ref):
    o_ref[...] = h_ref[...] + _dot(a_ref[...], wo_ref[...])


def _o_proj(h, a, wo, *, tt):
    t_rows, d = h.shape
    row = lambda i: (i, 0)
    return pl.pallas_call(
        _o_kernel,
        grid=(t_rows // tt,),
        in_specs=[pl.BlockSpec((tt, d), row), pl.BlockSpec((tt, d), row), pl.BlockSpec(wo.shape, lambda i: (0, 0))],
        out_specs=pl.BlockSpec((tt, d), row),
        out_shape=jax.ShapeDtypeStruct(h.shape, F32),
        input_output_aliases={0: 0},
        compiler_params=_params(("arbitrary",), VMEM_LIMIT_BYTES),
        name="o_proj",
    )(h, a, wo)


def _cumsum_kernel(lm_ref, lt_ref, cm_ref, ct_ref, *, seq, tc):
    cur = jnp.zeros((1, N_HEADS), F32)
    for r in range(N_META):
        cur = cur + lt_ref[r:r + 1, :]
        ct_ref[r:r + 1, :] = cur
    tril = (lax.broadcasted_iota(I32, (tc, tc), 1) <= lax.broadcasted_iota(I32, (tc, tc), 0)).astype(BF16)
    for c in range(seq // tc):
        blk = _dot_exact_rhs(tril, lm_ref[c * tc:(c + 1) * tc, :]) + cur
        cm_ref[c * tc:(c + 1) * tc, :] = blk
        cur = blk[tc - 1:tc, :]


def _cumsum(lf_main, lf_meta, *, n_batch, seq, tc):
    kern = functools.partial(_cumsum_kernel, seq=seq, tc=tc)
    return pl.pallas_call(
        kern,
        grid=(n_batch,),
        in_specs=[pl.BlockSpec((seq, N_HEADS), lambda b: (b, 0)), pl.BlockSpec((N_META, N_HEADS), lambda b: (b, 0))],
        out_specs=[pl.BlockSpec((seq, N_HEADS), lambda b: (b, 0)), pl.BlockSpec((N_META, N_HEADS), lambda b: (b, 0))],
        out_shape=[jax.ShapeDtypeStruct(lf_main.shape, F32), jax.ShapeDtypeStruct(lf_meta.shape, F32)],
        compiler_params=_params(("arbitrary",), VMEM_LIMIT_BYTES),
        name="logf_cumsum",
    )(lf_main, lf_meta)


def _attend(qh, k, v, cq, ck, mask, m, l, acc):
    s = _dot_nt(qh, k) + (cq - ck)
    if mask is not None:
        s = jnp.where(mask, s, NEG_INF)
    m_new = jnp.maximum(m, jnp.max(s, axis=1, keepdims=True))
    alpha = jnp.exp(m - m_new)
    p = jnp.exp(s - m_new)
    l = alpha * l + jnp.sum(p, axis=1, keepdims=True)
    acc = alpha * acc + _dot(p.astype(BF16), v)
    return m_new, l, acc


def _attn_kernel(qm_ref, qt_ref, km_ref, kt_ref, vm_ref, vt_ref, cqm_ref, cqt_ref, ckm_ref, ckt_ref,
                 om_ref, ot_ref, *, seq, tq):
    lane = lax.broadcasted_iota(I32, (1, LANES), 1)
    first = lane < HEAD_DIM
    kt = kt_ref[...]
    vt = vt_ref[...]

    def init(rows):
        return (jnp.full((rows, 1), NEG_INF, F32), jnp.zeros((rows, 1), F32), jnp.zeros((rows, LANES), F32))

    q2 = qt_ref[...]
    causal_t = (lax.broadcasted_iota(I32, (N_META, N_META), 1) <= lax.broadcasted_iota(I32, (N_META, N_META), 0))
    outs = []
    for hd in range(2):
        qh = jnp.where(first, q2, 0) if hd == 0 else jnp.where(first, 0, q2)
        m, l, acc = _attend(qh, kt, vt, cqt_ref[0, 0, :, hd:hd + 1], ckt_ref[0, 0, hd:hd + 1, :], causal_t, *init(N_META))
        outs.append(acc / l)
    ot_ref[...] = jnp.where(first, outs[0], outs[1]).astype(BF16)

    causal = (lax.broadcasted_iota(I32, (tq, tq), 1) <= lax.broadcasted_iota(I32, (tq, tq), 0))
    for qi in range(seq // tq):
        r0 = qi * tq
        q2 = qm_ref[r0:r0 + tq, :]
        outs = []
        for hd in range(2):
            qh = jnp.where(first, q2, 0) if hd == 0 else jnp.where(first, 0, q2)
            cq = cqm_ref[0, r0:r0 + tq, hd:hd + 1]
            state = _attend(qh, kt, vt, cq, ckt_ref[0, 0, hd:hd + 1, :], None, *init(tq))

            def body(j, st, qh=qh, cq=cq, hd=hd):
                c0 = pl.multiple_of(j * tq, tq)
                ck = ckm_ref[0, j][hd:hd + 1, :]
                return _attend(qh, km_ref[pl.ds(c0, tq), :], vm_ref[pl.ds(c0, tq), :], cq, ck, None, *st)

            state = lax.fori_loop(0, qi, body, state)
            ck = ckm_ref[0, qi][hd:hd + 1, :]
            m, l, acc = _attend(qh, km_ref[r0:r0 + tq, :], vm_ref[r0:r0 + tq, :], cq, ck, causal, *state)
            outs.append(acc / l)
        om_ref[r0:r0 + tq, :] = jnp.where(first, outs[0], outs[1]).astype(BF16)


def _prompt_attn(q, k, v, cq_main, cq_meta, ck_main, ck_meta, *, n_batch, seq, tq):
    t_rows, d = q.shape
    n_pairs = d // LANES
    meta0 = (n_batch * seq) // N_META
    main = pl.BlockSpec((seq, LANES), lambda b, p: (b, p))
    meta = pl.BlockSpec((N_META, LANES), lambda b, p: (meta0 + b, p))
    kern = functools.partial(_attn_kernel, seq=seq, tq=tq)
    return pl.pallas_call(
        kern,
        grid=(n_batch, n_pairs),
        in_specs=[main, meta, main, meta, main, meta,
                  pl.BlockSpec((1, seq, 2), lambda b, p: (p, b, 0)),
                  pl.BlockSpec((1, 1, N_META, 2), lambda b, p: (b, p, 0, 0)),
                  pl.BlockSpec((1, seq // tq, 2, tq), lambda b, p: (p, b, 0, 0)),
                  pl.BlockSpec((1, 1, 2, N_META), lambda b, p: (b, p, 0, 0))],
        out_specs=[main, pl.BlockSpec((N_META, LANES), lambda b, p: (b, p))],
        out_shape=[jax.ShapeDtypeStruct((n_batch * seq, d), BF16), jax.ShapeDtypeStruct((n_batch * N_META, d), BF16)],
        compiler_params=_params(("arbitrary", "arbitrary"), VMEM_LIMIT_BYTES),
        name="prompt_attn",
    )(q, q, k, k, v, v, cq_main, cq_meta, ck_main, ck_meta)


def _decode_kernel(pt_ref, q_ref, kc_ref, vc_ref, lc_ref, kn_ref, vn_ref, ln_ref, o_ref,
                   qbd, csq, m_s, l_s, acc_s, run_s, *, n_pages, n_new):
    j = pl.program_id(1)
    d = q_ref.shape[2]
    rows = n_new * N_HEADS
    head_of_col = lax.broadcasted_iota(I32, (N_HEADS, d), 1) // HEAD_DIM
    head_of_row = lax.broadcasted_iota(I32, (N_HEADS, d), 0)
    diag = head_of_col == head_of_row

    @pl.when(j == 0)
    def _():
        for qi in range(n_new):
            qbd[qi * N_HEADS:(qi + 1) * N_HEADS, :] = jnp.where(diag, q_ref[0, qi:qi + 1, :].astype(F32), 0.0)
        cur = jnp.zeros((N_HEADS, 1), F32)
        for qi in range(n_new):
            cur = cur + ln_ref[0, :, qi:qi + 1]
            csq[qi * N_HEADS:(qi + 1) * N_HEADS, :] = cur
        m_s[...] = jnp.full(m_s.shape, NEG_INF, F32)
        l_s[...] = jnp.zeros(l_s.shape, F32)
        acc_s[...] = jnp.zeros(acc_s.shape, F32)
        run_s[...] = jnp.zeros(run_s.shape, F32)

    page = lc_ref.shape[2]
    lf = lc_ref[0]
    later = (lax.broadcasted_iota(I32, (page, page), 0) > lax.broadcasted_iota(I32, (page, page), 1)).astype(BF16)
    l0, l1, l2 = _split3(lf)
    suffix = _dot(l0, later) + (_dot(l1, later) + _dot(l2, later)) + run_s[...]
    run_s[...] = run_s[...] + jnp.sum(lf, axis=1, keepdims=True)
    bias = jnp.concatenate([suffix] * n_new, axis=0) + csq[...]
    s = _dot(qbd[...].astype(BF16), kc_ref[0].astype(BF16)) + bias
    m_new = jnp.maximum(m_s[...], jnp.max(s, axis=1, keepdims=True))
    alpha = jnp.exp(m_s[...] - m_new)
    p = jnp.exp(s - m_new)
    l_s[...] = alpha * l_s[...] + jnp.sum(p, axis=1, keepdims=True)
    acc_s[...] = alpha * acc_s[...] + _dot_nt(p.astype(BF16), vc_ref[0].astype(BF16))
    m_s[...] = m_new

    @pl.when(j == n_pages - 1)
    def _():
        row_q = lax.broadcasted_iota(I32, (rows, 1), 0) // N_HEADS
        m, l, acc = m_s[...], l_s[...], acc_s[...]
        qf = qbd[...]
        for jn in range(n_new):
            kn = kn_ref[0, jn:jn + 1, :]
            s = jnp.sum(qf * kn, axis=1, keepdims=True)
            cj = csq[jn * N_HEADS:(jn + 1) * N_HEADS, :]
            s = s + (csq[...] - jnp.concatenate([cj] * n_new, axis=0))
            s = jnp.where(row_q >= jn, s, NEG_INF)
            m_new = jnp.maximum(m, s)
            alpha = jnp.exp(m - m_new)
            p = jnp.exp(s - m_new)
            l = alpha * l + p
            acc = alpha * acc + p * vn_ref[0, jn:jn + 1, :]
            m = m_new
        out = acc / l
        for qi in range(n_new):
            blk = jnp.where(diag, out[qi * N_HEADS:(qi + 1) * N_HEADS, :], 0.0)
            o_ref[0, qi:qi + 1, :] = jnp.sum(blk, axis=0, keepdims=True).astype(o_ref.dtype)


def _decode_attn(page_table, q, kc, vc, lc, kn, vn, ln_t):
    n_seq, n_new, d = q.shape
    n_pages = page_table.shape[1]
    page = kc.shape[2]
    rows = n_new * N_HEADS
    kern = functools.partial(_decode_kernel, n_pages=n_pages, n_new=n_new)
    seq3 = lambda n, j, pt: (n, 0, 0)
    pg3 = lambda n, j, pt: (pt[n, n_pages - 1 - j], 0, 0)
    grid_spec = pltpu.PrefetchScalarGridSpec(
        num_scalar_prefetch=1,
        grid=(n_seq, n_pages),
        in_specs=[
            pl.BlockSpec((1, n_new, d), seq3),
            pl.BlockSpec((1, d, page), pg3),
            pl.BlockSpec((1, d, page), pg3),
            pl.BlockSpec((1, N_HEADS, page), pg3),
            pl.BlockSpec((1, n_new, d), seq3),
            pl.BlockSpec((1, n_new, d), seq3),
            pl.BlockSpec((1, N_HEADS, n_new), seq3),
        ],
        out_specs=pl.BlockSpec((1, n_new, d), seq3),
        scratch_shapes=[pltpu.VMEM((rows, d), F32), pltpu.VMEM((rows, 1), F32), pltpu.VMEM((rows, 1), F32),
                        pltpu.VMEM((rows, 1), F32), pltpu.VMEM((rows, d), F32), pltpu.VMEM((N_HEADS, 1), F32)],
    )
    return pl.pallas_call(
        kern,
        grid_spec=grid_spec,
        out_shape=jax.ShapeDtypeStruct((n_seq, n_new, d), F32),
        compiler_params=_params(("arbitrary", "arbitrary"), VMEM_LIMIT_BYTES),
        name="decode_attn",
    )(page_table, q, kc, vc, lc, kn, vn, ln_t)


def _pick(totals, cands):
    for c in cands:
        if all(t % c == 0 for t in totals):
            return c
    raise ValueError(f"no tile among {cands} divides {totals}")


def kernel(x_prompt, x_sample, state_pool, cache_k, cache_v, cache_logf, page_table, meta, pool_norm_g, pool_w,
           pool_scale, kv_norm_g, w_kvf, b_f, k_norm_g, attn_norm_g, w_q, q_norm_g, w_o, ffn_norm_g,
           w_router_group, w_router_expert, w_gate, w_up, w_down):
    n_batch, seq, d = x_prompt.shape
    n_seq, n_new, _ = x_sample.shape
    depth = ffn_norm_g.shape[0]
    n_a = pool_norm_g.shape[0]
    n_phys, page = cache_k.shape[0], cache_k.shape[1]
    past_len = page_table.shape[1] * page
    r_main, r_meta, r_s = n_batch * seq, n_batch * N_META, n_seq * n_new
    t_rows = r_main + r_meta + r_s
    assert d == N_HEADS * HEAD_DIM and d % (LANES * SUBLANES) == 0 and n_batch % SUBLANES == 0 and n_seq % SUBLANES == 0

    tt_pool = _pick((seq, t_rows), (512, 256, 128))
    tt_proj = _pick((t_rows,), (640, 512, 256, 128))
    blk_moe = _pick((t_rows,), (1280, 1664, 1024, 512, 256, 128))
    tq = _pick((seq,), (512, 256, 128))
    tc = _pick((seq,), (256, 128))

    xs_t = jnp.transpose(x_sample, (1, 0, 2))
    h = jnp.concatenate([x_prompt.reshape(r_main, d),
                         jnp.broadcast_to(meta[None], (n_batch, N_META, d)).reshape(r_meta, d),
                         xs_t.reshape(r_s, d)], axis=0)

    gg = (jnp.arange(MXU_DIM)[:, None] // HEAD_DIM == jnp.arange(MXU_DIM)[None, :] // HEAD_DIM).astype(BF16)
    state_t = jnp.transpose(state_pool, (0, 2, 1, 3))
    zero_prev = jnp.zeros((POOL_STATE, n_batch, d), F32)

    def put(hbuf, rows, start):
        return lax.dynamic_update_slice(hbuf, rows, (start, 0))

    def moe(hbuf, layer):
        wr = jnp.concatenate([w_router_group[layer].T,
                              jnp.transpose(w_router_expert[layer], (0, 2, 1)).reshape(N_EXPERTS, d),
                              jnp.zeros((32 - N_GROUPS - N_EXPERTS, d), F32)], axis=0)
        return _moe(hbuf, ffn_norm_g[layer][None], wr, w_gate[layer].astype(BF16), w_up[layer].astype(BF16),
                    w_down[layer].astype(BF16), blk=blk_moe, tm=128)

    tails_p, tails_s = [], []
    for layer in range(n_a):
        g = pool_norm_g[layer][None]
        sc = pool_scale[layer][None]
        w = pool_w[layer]
        h_meta = h[r_main:r_main + r_meta]
        h_s = h[r_main + r_meta:]
        h, tail = _pool_main(h, h_meta, g, w, sc, n_batch=n_batch, seq=seq, tt=tt_pool)
        meta_t = jnp.transpose(h_meta.reshape(n_batch, N_META, d), (1, 0, 2))
        y_meta, _ = _pool_small(meta_t, zero_prev, g, w, sc, pos0=0)
        y_s, u_s = _pool_small(h_s.reshape(n_new, n_seq, d), state_t[layer], g, w, sc, pos0=past_len)
        h = put(h, jnp.transpose(y_meta, (1, 0, 2)).reshape(r_meta, d), r_main)
        h = put(h, y_s.reshape(r_s, d), r_main + r_meta)
        tails_p.append(jnp.transpose(tail[:, 1:], (1, 0, 2)))
        tails_s.append(jnp.concatenate([state_t[layer][n_new:], u_s], axis=0))
        h = moe(h, layer)

    wkv = w_kvf[:, :2 * d].astype(BF16)
    wf = jnp.pad(w_kvf[:, 2 * d:], ((0, 0), (0, LANES - N_HEADS)))
    k_all, v_all, kb, vb, lf = _kv_proj(h, kv_norm_g[None], wkv, wf, b_f[None], jnp.tile(k_norm_g, N_HEADS)[None], gg,
                                        tt=tt_proj)
    c_main, c_meta = _cumsum(lf[:r_main], lf[r_main:r_main + r_meta], n_batch=n_batch, seq=seq, tc=tc)
    n_qt = seq // tq
    cq_main = jnp.transpose(c_main.reshape(r_main, N_HEADS // 2, 2), (1, 0, 2))
    ck_main = jnp.transpose(c_main.reshape(n_batch * n_qt, tq, N_HEADS // 2, 2), (2, 0, 3, 1))
    cq_meta = jnp.transpose(c_meta.reshape(n_batch, N_META, N_HEADS // 2, 2), (0, 2, 1, 3))
    ck_meta = jnp.transpose(c_meta.reshape(n_batch, N_META, N_HEADS // 2, 2), (0, 2, 3, 1))

    kc = jnp.transpose(cache_k, (0, 2, 3, 1)).reshape(n_phys, d, page)
    vc = jnp.transpose(cache_v, (0, 2, 3, 1)).reshape(n_phys, d, page)
    lc = jnp.transpose(cache_logf, (0, 2, 1))

    def to_seq_major(rows):
        return jnp.transpose(rows.reshape(n_new, n_seq, rows.shape[-1]), (1, 0, 2))

    s0 = r_main + r_meta
    k_s, v_s, lf_s = to_seq_major(k_all[s0:]), to_seq_major(v_all[s0:]), to_seq_major(lf[s0:])
    lf_s_t = jnp.transpose(lf_s, (0, 2, 1))

    for layer in range(n_a, depth):
        jb = layer - n_a
        q = _q_proj(h, attn_norm_g[jb][None], w_q[jb].astype(BF16), jnp.tile(q_norm_g[jb], N_HEADS)[None], gg, tt=tt_proj)
        a_main, a_meta = _prompt_attn(q, kb, vb, cq_main, cq_meta, ck_main, ck_meta, n_batch=n_batch, seq=seq, tq=tq)
        a_s = _decode_attn(page_table, to_seq_major(q[s0:]).astype(F32), kc, vc, lc, k_s, v_s, lf_s_t)
        a = jnp.concatenate([a_main, a_meta, jnp.transpose(a_s, (1, 0, 2)).reshape(r_s, d).astype(BF16)], axis=0)
        h = _o_proj(h, a, w_o[jb].astype(BF16), tt=tt_proj)
        h = moe(h, layer)

    def with_meta(main_rows, meta_rows):
        c = main_rows.shape[-1]
        return jnp.concatenate([meta_rows.reshape(n_batch, N_META, c), main_rows.reshape(n_batch, seq, c)], axis=1)

    y_prompt = h[:r_main].reshape(n_batch, seq, d)
    y_sample = to_seq_major(h[s0:])
    pool_state_prompt = jnp.transpose(jnp.stack(tails_p, axis=0), (0, 2, 1, 3))
    pool_state_sample = jnp.transpose(jnp.stack(tails_s, axis=0), (0, 2, 1, 3))
    k_p = with_meta(k_all[:r_main], k_all[r_main:s0]).reshape(n_batch, seq + N_META, N_HEADS, HEAD_DIM)
    v_p = with_meta(v_all[:r_main], v_all[r_main:s0]).reshape(n_batch, seq + N_META, N_HEADS, HEAD_DIM)
    logf_p = with_meta(lf[:r_main], lf[r_main:s0])
    return (y_prompt, y_sample, pool_state_prompt, pool_state_sample, k_p, v_p, logf_p,
            k_s.reshape(n_seq, n_new, N_HEADS, HEAD_DIM), v_s.reshape(n_seq, n_new, N_HEADS, HEAD_DIM), lf_s)
```

```python
import functools

import jax
import jax.numpy as jnp
from jax import lax
from jax.experimental import pallas as pl
from jax.experimental.pallas import tpu as pltpu

F32 = jnp.float32
BF16 = jnp.bfloat16
I32 = jnp.int32

N_META = 16
POOL_WINDOWS = (2, 4, 8, 16)
POOL_STATE = max(POOL_WINDOWS) - 1
N_HEADS = 16
HEAD_DIM = 64
N_GROUPS = 4
EXPERTS_PER_GROUP = 4
N_EXPERTS = N_GROUPS * EXPERTS_PER_GROUP
ATTN_SCALE = HEAD_DIM ** -0.5
EPS = 1e-6
NEG_INF = -1e30

LANES = 128
SUBLANES = 8
MXU_DIM = 256
VMEM_LIMIT_BYTES = 56 * 1024 * 1024


def _params(sem, vmem=None):
    return pltpu.CompilerParams(dimension_semantics=sem, vmem_limit_bytes=vmem)


def _rms(x, g):
    return x * lax.rsqrt(jnp.mean(x * x, axis=-1, keepdims=True) + EPS) * g


def _dot(a, b):
    return jnp.dot(a, b, preferred_element_type=F32)


def _dot_nt(a, b):
    return lax.dot_general(a, b, (((1,), (1,)), ((), ())), preferred_element_type=F32)


def _split2(a):
    hi = a.astype(BF16)
    lo = (a - hi.astype(F32)).astype(BF16)
    return hi, lo


def _split3(a):
    hi = a.astype(BF16)
    r = a - hi.astype(F32)
    mid = r.astype(BF16)
    lo = (r - mid.astype(F32)).astype(BF16)
    return hi, mid, lo


def _dot3(a, b):
    ah, al = _split2(a)
    bh, bl = _split2(b)
    return _dot(ah, bh) + (_dot(ah, bl) + _dot(al, bh))


def _dot_exact_rhs(mask_bf16, b):
    b0, b1, b2 = _split3(b)
    return _dot(mask_bf16, b0) + (_dot(mask_bf16, b1) + _dot(mask_bf16, b2))


def _head_norm(k, gg, gain):
    ksq = k * k
    hi, lo = _split2(ksq)
    parts = []
    for c in range(k.shape[1] // MXU_DIM):
        sl = slice(c * MXU_DIM, (c + 1) * MXU_DIM)
        parts.append(_dot(hi[:, sl], gg) + _dot(lo[:, sl], gg))
    ms = jnp.concatenate(parts, axis=1) * (1.0 / HEAD_DIM)
    return k * lax.rsqrt(ms + EPS) * gain


def _pool_main_kernel(h_ref, hm_ref, g_ref, w_ref, sc_ref, o_ref, tail_ref, ext_ref, *, tt, n_t):
    i = pl.program_id(1)
    g = g_ref[...]
    gd = h_ref.shape[1] // len(POOL_WINDOWS)

    @pl.when(i == 0)
    def _():
        ext_ref[0:N_META, :] = _rms(hm_ref[...], g)

    x = h_ref[...]
    u = _rms(x, g)
    ext_ref[N_META:N_META + tt, :] = u
    outs = []
    for gi, w in enumerate(POOL_WINDOWS):
        lo, hi = gi * gd, (gi + 1) * gd
        ug = u[:, lo:hi]
        acc = ug
        for k in range(1, w):
            acc = acc + ext_ref[N_META - k:N_META - k + tt, lo:hi]
        diff = acc * (1.0 / w) - ug
        outs.append(_dot3(diff, w_ref[gi]))
    o_ref[...] = x + jnp.concatenate(outs, axis=1) * sc_ref[...]
    ext_ref[0:N_META, :] = ext_ref[tt:tt + N_META, :]

    @pl.when(i == n_t - 1)
    def _():
        tail_ref[0] = u[tt - N_META:tt, :]


def _pool_main(h, h_meta, g, w, sc, *, n_batch, seq, tt):
    t_rows, d = h.shape
    n_t = seq // tt
    kern = functools.partial(_pool_main_kernel, tt=tt, n_t=n_t)
    return pl.pallas_call(
        kern,
        grid=(n_batch, n_t),
        in_specs=[
            pl.BlockSpec((tt, d), lambda b, i: (b * n_t + i, 0)),
            pl.BlockSpec((N_META, d), lambda b, i: (b, 0)),
            pl.BlockSpec((1, d), lambda b, i: (0, 0)),
            pl.BlockSpec(w.shape, lambda b, i: (0, 0, 0)),
            pl.BlockSpec((1, d), lambda b, i: (0, 0)),
        ],
        out_specs=[
            pl.BlockSpec((tt, d), lambda b, i: (b * n_t + i, 0)),
            pl.BlockSpec((1, N_META, d), lambda b, i: (b, 0, 0)),
        ],
        out_shape=[jax.ShapeDtypeStruct(h.shape, F32), jax.ShapeDtypeStruct((n_batch, N_META, d), F32)],
        scratch_shapes=[pltpu.VMEM((N_META + tt, d), F32)],
        input_output_aliases={0: 0},
        compiler_params=_params(("arbitrary", "arbitrary"), VMEM_LIMIT_BYTES),
        name="pool_main",
    )(h, h_meta, g, w, sc)


def _pool_small_kernel(x_ref, prev_ref, g_ref, w_ref, sc_ref, y_ref, u_ref, *, n_new, pos0):
    g = g_ref[...]
    n_seq, d = x_ref.shape[1], x_ref.shape[2]
    gd = d // len(POOL_WINDOWS)
    us = [_rms(x_ref[t], g) for t in range(n_new)]
    ext = [prev_ref[j] for j in range(POOL_STATE)] + us
    outs = []
    for gi, w in enumerate(POOL_WINDOWS):
        lo, hi = gi * gd, (gi + 1) * gd
        diffs = []
        for t in range(n_new):
            ug = us[t][:, lo:hi]
            acc = ug
            for k in range(1, w):
                acc = acc + ext[POOL_STATE + t - k][:, lo:hi]
            cnt = float(min(pos0 + t + 1, w))
            diffs.append(acc / cnt - ug)
        outs.append(_dot3(jnp.concatenate(diffs, axis=0), w_ref[gi]))
    out = jnp.concatenate(outs, axis=1) * sc_ref[...]
    for t in range(n_new):
        y_ref[t] = x_ref[t] + out[t * n_seq:(t + 1) * n_seq, :]
        u_ref[t] = us[t]


def _pool_small(x, prev, g, w, sc, *, pos0):
    n_new = x.shape[0]
    kern = functools.partial(_pool_small_kernel, n_new=n_new, pos0=pos0)
    return pl.pallas_call(
        kern,
        out_shape=[jax.ShapeDtypeStruct(x.shape, F32), jax.ShapeDtypeStruct(x.shape, F32)],
        compiler_params=_params(None, VMEM_LIMIT_BYTES),
        name="pool_small",
    )(x, prev, g, w, sc)


def _moe_kernel(h_ref, g_ref, wr_ref, wg_ref, wu_ref, wd_ref, o_ref,
                u2d, slots, x_t, o_t, rt_i, gates, cnt_v, rt_si, cnt_s, lst_src, lst_dst,
                *, blk, tm, pitch, cap):
    e = pl.program_id(1)
    n_chunks = blk // LANES
    d = h_ref.shape[1]
    n_slab = d // LANES

    @pl.when(e == 0)
    def _router():
        g = g_ref[...]
        wr_hi, wr_lo = _split2(wr_ref[...])
        sub16 = lax.broadcasted_iota(I32, (N_EXPERTS, LANES), 0)
        tri = (lax.broadcasted_iota(I32, (LANES, LANES), 0)
               < lax.broadcasted_iota(I32, (LANES, LANES), 1)).astype(BF16)

        def chunk(c, carry):
            r0 = pl.multiple_of(c * LANES, LANES)
            u = _rms(h_ref[pl.ds(r0, LANES), :], g)
            for j in range(n_slab):
                u2d[pl.ds(c * (LANES * n_slab) + j, LANES, stride=n_slab), :] = u[:, j * LANES:(j + 1) * LANES]
            u_hi, u_lo = _split2(u)
            lt = _dot_nt(wr_hi, u_hi) + (_dot_nt(wr_hi, u_lo) + _dot_nt(wr_lo, u_hi))
            gl = [lt[k:k + 1, :] for k in range(N_GROUPS)]
            gmax = jnp.maximum(jnp.maximum(gl[0], gl[1]), jnp.maximum(gl[2], gl[3]))
            gsel = jnp.where(gl[0] >= gmax, 0, jnp.where(gl[1] >= gmax, 1, jnp.where(gl[2] >= gmax, 2, 3)))
            denom = (jnp.exp(gl[0] - gmax) + jnp.exp(gl[1] - gmax)) + (jnp.exp(gl[2] - gmax) + jnp.exp(gl[3] - gmax))
            p_top = 1.0 / denom
            le = []
            for k in range(EXPERTS_PER_GROUP):
                rows = [lt[N_GROUPS + gi * EXPERTS_PER_GROUP + k:N_GROUPS + gi * EXPERTS_PER_GROUP + k + 1, :]
                        for gi in range(N_GROUPS)]
                le.append(jnp.where(gsel == 0, rows[0], jnp.where(gsel == 1, rows[1],
                                                                   jnp.where(gsel == 2, rows[2], rows[3]))))
            v1 = jnp.maximum(jnp.maximum(le[0], le[1]), jnp.maximum(le[2], le[3]))
            i1 = jnp.where(le[0] >= v1, 0, jnp.where(le[1] >= v1, 1, jnp.where(le[2] >= v1, 2, 3)))
            rest = [jnp.where(i1 == k, NEG_INF, le[k]) for k in range(EXPERTS_PER_GROUP)]
            v2 = jnp.maximum(jnp.maximum(rest[0], rest[1]), jnp.maximum(rest[2], rest[3]))
            i2 = jnp.where((rest[0] >= v2) & (i1 != 0), 0,
                           jnp.where((rest[1] >= v2) & (i1 != 1), 1,
                                     jnp.where((rest[2] >= v2) & (i1 != 2), 2, 3)))
            ex = jnp.exp(v2 - v1)
            gate1 = p_top / (1.0 + ex)
            gate2 = p_top * ex / (1.0 + ex)
            e1 = gsel * EXPERTS_PER_GROUP + i1
            e2 = gsel * EXPERTS_PER_GROUP + i2
            oh1 = (sub16 == e1).astype(F32)
            oh2 = (sub16 == e2).astype(F32)
            both = oh1 + oh2
            rank = _dot(both.astype(BF16), tri) + carry
            pos1 = jnp.sum(oh1 * rank, axis=0, keepdims=True).astype(I32)
            pos2 = jnp.sum(oh2 * rank, axis=0, keepdims=True).astype(I32)
            rt_i[c] = jnp.zeros((SUBLANES, LANES), I32)
            gates[c] = jnp.zeros((SUBLANES, LANES), F32)
            rt_i[c, 0:1, :] = e1 * cap + pos1
            rt_i[c, 1:2, :] = e2 * cap + pos2
            gates[c, 0:1, :] = gate1
            gates[c, 1:2, :] = gate2
            return carry + jnp.sum(both, axis=1, keepdims=True)

        total = lax.fori_loop(0, n_chunks, chunk, jnp.zeros((N_EXPERTS, 1), F32))
        cnt_v[...] = jnp.broadcast_to(total, (N_EXPERTS, LANES)).astype(I32)
        pltpu.sync_copy(rt_i, rt_si)
        pltpu.sync_copy(cnt_v, cnt_s)

        def invert(c, _):
            for l in range(LANES):
                row = (c * LANES + l) * n_slab
                a1 = rt_si[c, 0, l]
                a2 = rt_si[c, 1, l]
                lst_src[a1] = row
                lst_src[a2] = row
                lst_dst[a1] = row
                lst_dst[a2] = row + blk * n_slab
            return 0

        lax.fori_loop(0, n_chunks, invert, 0)

        for ex in range(N_EXPERTS):
            n = cnt_s[ex, 0]
            n_pad = ((n + (tm - 1)) // tm) * tm

            def pad(i, _, ex=ex, n=n):
                lst_src[ex * cap + i] = lst_src[ex * cap + n - 1]
                lst_dst[ex * cap + i] = lst_dst[ex * cap + n - 1]
                return 0

            lax.fori_loop(n, n_pad, pad, 0)

    n_e = cnt_s[e, 0]
    n_tiles = (n_e + (tm - 1)) // tm

    def tile(ti, _):
        base = e * cap + ti * tm
        for r in range(tm):
            src = pl.multiple_of(lst_src[base + r], n_slab)
            x_t[pl.ds(r, n_slab, stride=pitch), :] = u2d[pl.ds(src, n_slab), :]
        x = jnp.concatenate([x_t[j * pitch:j * pitch + tm, :] for j in range(n_slab)], axis=1).astype(BF16)
        hg = _dot(x, wg_ref[0])
        hu = _dot(x, wu_ref[0])
        act = (hg * jax.nn.sigmoid(hg) * hu).astype(BF16)
        o = _dot(act, wd_ref[0])
        for j in range(n_slab):
            o_t[j * pitch:j * pitch + tm, :] = o[:, j * LANES:(j + 1) * LANES]
        for r in range(tm):
            dst = pl.multiple_of(lst_dst[base + r], n_slab)
            slots[pl.ds(dst, n_slab), :] = o_t[pl.ds(r, n_slab, stride=pitch), :]
        return 0

    lax.fori_loop(0, n_tiles, tile, 0)

    @pl.when(e == N_EXPERTS - 1)
    def _combine():
        eye = (lax.broadcasted_iota(I32, (LANES, LANES), 0)
               == lax.broadcasted_iota(I32, (LANES, LANES), 1)).astype(BF16)

        def chunk(c, _):
            r0 = pl.multiple_of(c * LANES, LANES)
            g0, g1, g2 = _split3(gates[c])
            gcol = _dot_nt(eye, g0) + (_dot_nt(eye, g1) + _dot_nt(eye, g2))
            ga, gb = gcol[:, 0:1], gcol[:, 1:2]
            for j in range(n_slab):
                a = slots[pl.ds(c * (LANES * n_slab) + j, LANES, stride=n_slab), :]
                b = slots[pl.ds((blk + c * LANES) * n_slab + j, LANES, stride=n_slab), :]
                o_ref[pl.ds(r0, LANES), j * LANES:(j + 1) * LANES] = (
                    h_ref[pl.ds(r0, LANES), j * LANES:(j + 1) * LANES] + (ga * a + gb * b))
            return 0

        lax.fori_loop(0, n_chunks, chunk, 0)


def _moe(h, g, wr_t, wg, wu, wd, *, blk, tm):
    t_rows, d = h.shape
    f = wg.shape[2]
    n_blocks = t_rows // blk
    n_chunks = blk // LANES
    n_slab = d // LANES
    pitch = tm + SUBLANES
    cap = -(-blk // tm) * tm
    kern = functools.partial(_moe_kernel, blk=blk, tm=tm, pitch=pitch, cap=cap)
    return pl.pallas_call(
        kern,
        grid=(n_blocks, N_EXPERTS),
        in_specs=[
            pl.BlockSpec((blk, d), lambda i, e: (i, 0)),
            pl.BlockSpec((1, d), lambda i, e: (0, 0)),
            pl.BlockSpec(wr_t.shape, lambda i, e: (0, 0)),
            pl.BlockSpec((1, d, f), lambda i, e: (e, 0, 0)),
            pl.BlockSpec((1, d, f), lambda i, e: (e, 0, 0)),
            pl.BlockSpec((1, f, d), lambda i, e: (e, 0, 0)),
        ],
        out_specs=pl.BlockSpec((blk, d), lambda i, e: (i, 0), pipeline_mode=pl.Buffered(1)),
        out_shape=jax.ShapeDtypeStruct(h.shape, F32),
        scratch_shapes=[
            pltpu.VMEM((blk * n_slab, LANES), F32),
            pltpu.VMEM((2 * blk * n_slab, LANES), F32),
            pltpu.VMEM((n_slab * pitch, LANES), F32),
            pltpu.VMEM((n_slab * pitch, LANES), F32),
            pltpu.VMEM((n_chunks, SUBLANES, LANES), I32),
            pltpu.VMEM((n_chunks, SUBLANES, LANES), F32),
            pltpu.VMEM((N_EXPERTS, LANES), I32),
            pltpu.SMEM((n_chunks, SUBLANES, LANES), I32),
            pltpu.SMEM((N_EXPERTS, LANES), I32),
            pltpu.SMEM((N_EXPERTS * cap,), I32),
            pltpu.SMEM((N_EXPERTS * cap,), I32),
        ],
        input_output_aliases={0: 0},
        compiler_params=_params(("arbitrary", "arbitrary"), VMEM_LIMIT_BYTES),
        name="moe",
    )(h, g, wr_t, wg, wu, wd)


def _kv_kernel(h_ref, g_ref, wkv_ref, wf_ref, bf_ref, kg_ref, gg_ref, k_ref, v_ref, kb_ref, vb_ref, lf_ref):
    d = h_ref.shape[1]
    u = _rms(h_ref[...], g_ref[...])
    p = _dot(u.astype(BF16), wkv_ref[...])
    kn = _head_norm(p[:, :d], gg_ref[...], kg_ref[...])
    v = p[:, d:]
    k_ref[...] = kn
    v_ref[...] = v
    kb_ref[...] = kn.astype(BF16)
    vb_ref[...] = v.astype(BF16)
    z = _dot3(u, wf_ref[...])[:, :N_HEADS] + bf_ref[...]
    lf_ref[...] = jnp.minimum(z, 0.0) - jnp.log1p(jnp.exp(-jnp.abs(z)))


def _kv_proj(h, g, wkv, wf, bfr, kg, gg, *, tt):
    t_rows, d = h.shape
    row = lambda i: (i, 0)
    const = lambda i: (0, 0)
    return pl.pallas_call(
        _kv_kernel,
        grid=(t_rows // tt,),
        in_specs=[
            pl.BlockSpec((tt, d), row),
            pl.BlockSpec((1, d), const),
            pl.BlockSpec(wkv.shape, const),
            pl.BlockSpec(wf.shape, const),
            pl.BlockSpec((1, N_HEADS), const),
            pl.BlockSpec((1, d), const),
            pl.BlockSpec(gg.shape, const),
        ],
        out_specs=[pl.BlockSpec((tt, d), row)] * 4 + [pl.BlockSpec((tt, N_HEADS), row)],
        out_shape=[jax.ShapeDtypeStruct((t_rows, d), F32), jax.ShapeDtypeStruct((t_rows, d), F32),
                   jax.ShapeDtypeStruct((t_rows, d), BF16), jax.ShapeDtypeStruct((t_rows, d), BF16),
                   jax.ShapeDtypeStruct((t_rows, N_HEADS), F32)],
        compiler_params=_params(("arbitrary",), VMEM_LIMIT_BYTES),
        name="kv_proj",
    )(h, g, wkv, wf, bfr, kg, gg)


def _q_kernel(h_ref, g_ref, wq_ref, qg_ref, gg_ref, q_ref):
    u = _rms(h_ref[...], g_ref[...])
    q = _dot(u.astype(BF16), wq_ref[...])
    q_ref[...] = (_head_norm(q, gg_ref[...], qg_ref[...]) * ATTN_SCALE).astype(BF16)


def _q_proj(h, g, wq, qg, gg, *, tt):
    t_rows, d = h.shape
    row = lambda i: (i, 0)
    const = lambda i: (0, 0)
    return pl.pallas_call(
        _q_kernel,
        grid=(t_rows // tt,),
        in_specs=[pl.BlockSpec((tt, d), row), pl.BlockSpec((1, d), const), pl.BlockSpec(wq.shape, const),
                  pl.BlockSpec((1, d), const), pl.BlockSpec(gg.shape, const)],
        out_specs=pl.BlockSpec((tt, d), row),
        out_shape=jax.ShapeDtypeStruct((t_rows, d), BF16),
        compiler_params=_params(("arbitrary",), VMEM_LIMIT_BYTES),
        name="q_proj",
    )(h, g, wq, qg, gg)


def _o_kernel(h_ref, a_ref, wo_ref, o_ref):
    o_ref[...] = h_ref[...] + _dot(a_ref[...], wo_ref[...])


def _o_proj(h, a, wo, *, tt):
    t_rows, d = h.shape
    row = lambda i: (i, 0)
    return pl.pallas_call(
        _o_kernel,
        grid=(t_rows // tt,),
        in_specs=[pl.BlockSpec((tt, d), row), pl.BlockSpec((tt, d), row), pl.BlockSpec(wo.shape, lambda i: (0, 0))],
        out_specs=pl.BlockSpec((tt, d), row),
        out_shape=jax.ShapeDtypeStruct(h.shape, F32),
        input_output_aliases={0: 0},
        compiler_params=_params(("arbitrary",), VMEM_LIMIT_BYTES),
        name="o_proj",
    )(h, a, wo)


def _cumsum_kernel(lm_ref, lt_ref, cm_ref, ct_ref, *, seq, tc):
    cur = jnp.zeros((1, N_HEADS), F32)
    for r in range(N_META):
        cur = cur + lt_ref[r:r + 1, :]
        ct_ref[r:r + 1, :] = cur
    tril = (lax.broadcasted_iota(I32, (tc, tc), 1) <= lax.broadcasted_iota(I32, (tc, tc), 0)).astype(BF16)
    for c in range(seq // tc):
        blk = _dot_exact_rhs(tril, lm_ref[c * tc:(c + 1) * tc, :]) + cur
        cm_ref[c * tc:(c + 1) * tc, :] = blk
        cur = blk[tc - 1:tc, :]


def _cumsum(lf_main, lf_meta, *, n_batch, seq, tc):
    kern = functools.partial(_cumsum_kernel, seq=seq, tc=tc)
    return pl.pallas_call(
        kern,
        grid=(n_batch,),
        in_specs=[pl.BlockSpec((seq, N_HEADS), lambda b: (b, 0)), pl.BlockSpec((N_META, N_HEADS), lambda b: (b, 0))],
        out_specs=[pl.BlockSpec((seq, N_HEADS), lambda b: (b, 0)), pl.BlockSpec((N_META, N_HEADS), lambda b: (b, 0))],
        out_shape=[jax.ShapeDtypeStruct(lf_main.shape, F32), jax.ShapeDtypeStruct(lf_meta.shape, F32)],
        compiler_params=_params(("arbitrary",), VMEM_LIMIT_BYTES),
        name="logf_cumsum",
    )(lf_main, lf_meta)


def _attend(qh, k, v, cq, ck, mask, m, l, acc):
    s = _dot_nt(qh, k) + (cq - ck)
    if mask is not None:
        s = jnp.where(mask, s, NEG_INF)
    m_new = jnp.maximum(m, jnp.max(s, axis=1, keepdims=True))
    alpha = jnp.exp(m - m_new)
    p = jnp.exp(s - m_new)
    l = alpha * l + jnp.sum(p, axis=1, keepdims=True)
    acc = alpha * acc + _dot(p.astype(BF16), v)
    return m_new, l, acc


def _attn_kernel(qm_ref, qt_ref, km_ref, kt_ref, vm_ref, vt_ref, cqm_ref, cqt_ref, ckm_ref, ckt_ref,
                 om_ref, ot_ref, *, seq, tq):
    lane = lax.broadcasted_iota(I32, (1, LANES), 1)
    first = lane < HEAD_DIM
    kt = kt_ref[...]
    vt = vt_ref[...]

    def init(rows):
        return (jnp.full((rows, 1), NEG_INF, F32), jnp.zeros((rows, 1), F32), jnp.zeros((rows, LANES), F32))

    q2 = qt_ref[...]
    causal_t = (lax.broadcasted_iota(I32, (N_META, N_META), 1) <= lax.broadcasted_iota(I32, (N_META, N_META), 0))
    outs = []
    for hd in range(2):
        qh = jnp.where(first, q2, 0) if hd == 0 else jnp.where(first, 0, q2)
        m, l, acc = _attend(qh, kt, vt, cqt_ref[0, 0, :, hd:hd + 1], ckt_ref[0, 0, hd:hd + 1, :], causal_t, *init(N_META))
        outs.append(acc / l)
    ot_ref[...] = jnp.where(first, outs[0], outs[1]).astype(BF16)

    causal = (lax.broadcasted_iota(I32, (tq, tq), 1) <= lax.broadcasted_iota(I32, (tq, tq), 0))
    for qi in range(seq // tq):
        r0 = qi * tq
        q2 = qm_ref[r0:r0 + tq, :]
        outs = []
        for hd in range(2):
            qh = jnp.where(first, q2, 0) if hd == 0 else jnp.where(first, 0, q2)
            cq = cqm_ref[0, r0:r0 + tq, hd:hd + 1]
            state = _attend(qh, kt, vt, cq, ckt_ref[0, 0, hd:hd + 1, :], None, *init(tq))

            def body(j, st, qh=qh, cq=cq, hd=hd):
                c0 = pl.multiple_of(j * tq, tq)
                ck = ckm_ref[0, j][hd:hd + 1, :]
                return _attend(qh, km_ref[pl.ds(c0, tq), :], vm_ref[pl.ds(c0, tq), :], cq, ck, None, *st)

            state = lax.fori_loop(0, qi, body, state)
            ck = ckm_ref[0, qi][hd:hd + 1, :]
            m, l, acc = _attend(qh, km_ref[r0:r0 + tq, :], vm_ref[r0:r0 + tq, :], cq, ck, causal, *state)
            outs.append(acc / l)
        om_ref[r0:r0 + tq, :] = jnp.where(first, outs[0], outs[1]).astype(BF16)


def _prompt_attn(q, k, v, cq_main, cq_meta, ck_main, ck_meta, *, n_batch, seq, tq):
    t_rows, d = q.shape
    n_pairs = d // LANES
    meta0 = (n_batch * seq) // N_META
    main = pl.BlockSpec((seq, LANES), lambda b, p: (b, p))
    meta = pl.BlockSpec((N_META, LANES), lambda b, p: (meta0 + b, p))
    kern = functools.partial(_attn_kernel, seq=seq, tq=tq)
    return pl.pallas_call(
        kern,
        grid=(n_batch, n_pairs),
        in_specs=[main, meta, main, meta, main, meta,
                  pl.BlockSpec((1, seq, 2), lambda b, p: (p, b, 0)),
                  pl.BlockSpec((1, 1, N_META, 2), lambda b, p: (b, p, 0, 0)),
                  pl.BlockSpec((1, seq // tq, 2, tq), lambda b, p: (p, b, 0, 0)),
                  pl.BlockSpec((1, 1, 2, N_META), lambda b, p: (b, p, 0, 0))],
        out_specs=[main, pl.BlockSpec((N_META, LANES), lambda b, p: (b, p))],
        out_shape=[jax.ShapeDtypeStruct((n_batch * seq, d), BF16), jax.ShapeDtypeStruct((n_batch * N_META, d), BF16)],
        compiler_params=_params(("arbitrary", "arbitrary"), VMEM_LIMIT_BYTES),
        name="prompt_attn",
    )(q, q, k, k, v, v, cq_main, cq_meta, ck_main, ck_meta)


def _decode_kernel(pt_ref, q_ref, *refs, n_steps, n_new, ppg):
    kc_refs, vc_refs, lc_refs = refs[:ppg], refs[ppg:2 * ppg], refs[2 * ppg:3 * ppg]
    kn_ref, vn_ref, ln_ref, o_ref, qbd, qbd_b, csq, m_s, l_s, acc_s, run_s = refs[3 * ppg:]
    j = pl.program_id(1)
    d = q_ref.shape[2]
    rows = n_new * N_HEADS
    head_of_col = lax.broadcasted_iota(I32, (N_HEADS, d), 1) // HEAD_DIM
    head_of_row = lax.broadcasted_iota(I32, (N_HEADS, d), 0)
    diag = head_of_col == head_of_row

    @pl.when(j == 0)
    def _():
        for qi in range(n_new):
            qbd[qi * N_HEADS:(qi + 1) * N_HEADS, :] = jnp.where(diag, q_ref[0, qi:qi + 1, :].astype(F32), 0.0)
        qbd_b[...] = qbd[...].astype(BF16)
        cur = jnp.zeros((N_HEADS, 1), F32)
        for qi in range(n_new):
            cur = cur + ln_ref[0, :, qi:qi + 1]
            csq[qi * N_HEADS:(qi + 1) * N_HEADS, :] = cur
        m_s[...] = jnp.full(m_s.shape, NEG_INF, F32)
        l_s[...] = jnp.zeros(l_s.shape, F32)
        acc_s[...] = jnp.zeros(acc_s.shape, F32)
        run_s[...] = jnp.zeros(run_s.shape, F32)

    page = lc_refs[0].shape[2]
    later = (lax.broadcasted_iota(I32, (page, page), 0) > lax.broadcasted_iota(I32, (page, page), 1)).astype(BF16)
    qb = qbd_b[...]
    run = run_s[...]
    scores = []
    for pg in range(ppg):
        lf = lc_refs[pg][0]
        l0, l1, l2 = _split3(lf)
        suffix = _dot(l0, later) + (_dot(l1, later) + _dot(l2, later)) + run
        run = run + jnp.sum(lf, axis=1, keepdims=True)
        scores.append(_dot(qb, kc_refs[pg][0].astype(BF16)) + jnp.concatenate([suffix] * n_new, axis=0))
    run_s[...] = run
    s = jnp.concatenate(scores, axis=1) + csq[...]
    m_new = jnp.maximum(m_s[...], jnp.max(s, axis=1, keepdims=True))
    alpha = jnp.exp(m_s[...] - m_new)
    p32 = jnp.exp(s - m_new)
    l_s[...] = alpha * l_s[...] + jnp.sum(p32, axis=1, keepdims=True)
    p = p32.astype(BF16)
    pv =_dot_nt(p[:, 0:page], vc_refs[0][0].astype(BF16))
    for pg in range(1, ppg):
        pv = pv + _dot_nt(p[:, pg * page:(pg + 1) * page], vc_refs[pg][0].astype(BF16))
    acc_s[...] = alpha * acc_s[...] + pv
    m_s[...] = m_new

    @pl.when(j == n_steps - 1)
    def _():
        row_q = lax.broadcasted_iota(I32, (rows, 1), 0) // N_HEADS
        m, l, acc = m_s[...], l_s[...], acc_s[...]
        qf = qbd[...]
        for jn in range(n_new):
            kn = kn_ref[0, jn:jn + 1, :]
            s = jnp.sum(qf * kn, axis=1, keepdims=True)
            cj = csq[jn * N_HEADS:(jn + 1) * N_HEADS, :]
            s = s + (csq[...] - jnp.concatenate([cj] * n_new, axis=0))
            s = jnp.where(row_q >= jn, s, NEG_INF)
            m_new = jnp.maximum(m, s)
            alpha = jnp.exp(m - m_new)
            p = jnp.exp(s - m_new)
            l = alpha * l + p
            acc = alpha * acc + p * vn_ref[0, jn:jn + 1, :]
            m = m_new
        out = acc / l
        for qi in range(n_new):
            blk = jnp.where(diag, out[qi * N_HEADS:(qi + 1) * N_HEADS, :], 0.0)
            o_ref[0, qi:qi + 1, :] = jnp.sum(blk, axis=0, keepdims=True).astype(o_ref.dtype)


def _decode_attn(page_table, q, kc, vc, lc, kn, vn, ln_t):
    n_seq, n_new, d = q.shape
    n_pages = page_table.shape[1]
    page = kc.shape[2]
    rows = n_new * N_HEADS
    ppg = _pick((n_pages,), (8, 4, 2, 1))
    n_steps = n_pages // ppg
    kern = functools.partial(_decode_kernel, n_steps=n_steps, n_new=n_new, ppg=ppg)
    seq3 = lambda n, j, pt: (n, 0, 0)

    def pg3(pg):
        return lambda n, j, pt: (pt[n, n_pages - 1 - (j * ppg + pg)], 0, 0)

    grid_spec = pltpu.PrefetchScalarGridSpec(
        num_scalar_prefetch=1,
        grid=(n_seq, n_steps),
        in_specs=([pl.BlockSpec((1, n_new, d), seq3)]
                  + [pl.BlockSpec((1, d, page), pg3(pg)) for pg in range(ppg)]
                  + [pl.BlockSpec((1, d, page), pg3(pg)) for pg in range(ppg)]
                  + [pl.BlockSpec((1, N_HEADS, page), pg3(pg)) for pg in range(ppg)]
                  + [pl.BlockSpec((1, n_new, d), seq3), pl.BlockSpec((1, n_new, d), seq3),
                     pl.BlockSpec((1, N_HEADS, n_new), seq3)]),
        out_specs=pl.BlockSpec((1, n_new, d), seq3),
        scratch_shapes=[pltpu.VMEM((rows, d), F32), pltpu.VMEM((rows, d), BF16), pltpu.VMEM((rows, 1), F32),
                        pltpu.VMEM((rows, 1), F32), pltpu.VMEM((rows, 1), F32), pltpu.VMEM((rows, d), F32),
                        pltpu.VMEM((N_HEADS, 1), F32)],
    )
    return pl.pallas_call(
        kern,
        grid_spec=grid_spec,
        out_shape=jax.ShapeDtypeStruct((n_seq, n_new, d), F32),
        compiler_params=_params(("arbitrary", "arbitrary"), VMEM_LIMIT_BYTES),
        name="decode_attn",
    )(page_table, q, *([kc] * ppg), *([vc] * ppg), *([lc] * ppg), kn, vn, ln_t)


def _pick(totals, cands):
    for c in cands:
        if all(t % c == 0 for t in totals):
            return c
    raise ValueError(f"no tile among {cands} divides {totals}")


def kernel(x_prompt, x_sample, state_pool, cache_k, cache_v, cache_logf, page_table, meta, pool_norm_g, pool_w,
           pool_scale, kv_norm_g, w_kvf, b_f, k_norm_g, attn_norm_g, w_q, q_norm_g, w_o, ffn_norm_g,
           w_router_group, w_router_expert, w_gate, w_up, w_down):
    n_batch, seq, d = x_prompt.shape
    n_seq, n_new, _ = x_sample.shape
    depth = ffn_norm_g.shape[0]
    n_a = pool_norm_g.shape[0]
    n_phys, page = cache_k.shape[0], cache_k.shape[1]
    past_len = page_table.shape[1] * page
    r_main, r_meta, r_s = n_batch * seq, n_batch * N_META, n_seq * n_new
    t_rows = r_main + r_meta + r_s
    assert d == N_HEADS * HEAD_DIM and d % (LANES * SUBLANES) == 0 and n_batch % SUBLANES == 0 and n_seq % SUBLANES == 0

    tt_pool = _pick((seq, t_rows), (512, 256, 128))
    tt_proj = _pick((t_rows,), (640, 512, 256, 128))
    blk_moe = _pick((t_rows,), (1664, 1280, 1024, 512, 256, 128))
    tq = _pick((seq,), (512, 256, 128))
    tc = _pick((seq,), (256, 128))

    xs_t = jnp.transpose(x_sample, (1, 0, 2))
    h = jnp.concatenate([x_prompt.reshape(r_main, d),
                         jnp.broadcast_to(meta[None], (n_batch, N_META, d)).reshape(r_meta, d),
                         xs_t.reshape(r_s, d)], axis=0)

    gg = (jnp.arange(MXU_DIM)[:, None] // HEAD_DIM == jnp.arange(MXU_DIM)[None, :] // HEAD_DIM).astype(BF16)
    state_t = jnp.transpose(state_pool, (0, 2, 1, 3))
    zero_prev = jnp.zeros((POOL_STATE, n_batch, d), F32)

    def put(hbuf, rows, start):
        return lax.dynamic_update_slice(hbuf, rows, (start, 0))

    def moe(hbuf, layer):
        wr = jnp.concatenate([w_router_group[layer].T,
                              jnp.transpose(w_router_expert[layer], (0, 2, 1)).reshape(N_EXPERTS, d),
                              jnp.zeros((32 - N_GROUPS - N_EXPERTS, d), F32)], axis=0)
        return _moe(hbuf, ffn_norm_g[layer][None], wr, w_gate[layer].astype(BF16), w_up[layer].astype(BF16),
                    w_down[layer].astype(BF16), blk=blk_moe, tm=128)

    tails_p, tails_s = [], []
    for layer in range(n_a):
        g = pool_norm_g[layer][None]
        sc = pool_scale[layer][None]
        w = pool_w[layer]
        h_meta = h[r_main:r_main + r_meta]
        h_s = h[r_main + r_meta:]
        h, tail = _pool_main(h, h_meta, g, w, sc, n_batch=n_batch, seq=seq, tt=tt_pool)
        meta_t = jnp.transpose(h_meta.reshape(n_batch, N_META, d), (1, 0, 2))
        y_meta, _ = _pool_small(meta_t, zero_prev, g, w, sc, pos0=0)
        y_s, u_s = _pool_small(h_s.reshape(n_new, n_seq, d), state_t[layer], g, w, sc, pos0=past_len)
        h = put(h, jnp.transpose(y_meta, (1, 0, 2)).reshape(r_meta, d), r_main)
        h = put(h, y_s.reshape(r_s, d), r_main + r_meta)
        tails_p.append(jnp.transpose(tail[:, 1:], (1, 0, 2)))
        tails_s.append(jnp.concatenate([state_t[layer][n_new:], u_s], axis=0))
        h = moe(h, layer)

    wkv = w_kvf[:, :2 * d].astype(BF16)
    wf = jnp.pad(w_kvf[:, 2 * d:], ((0, 0), (0, LANES - N_HEADS)))
    k_all, v_all, kb, vb, lf = _kv_proj(h, kv_norm_g[None], wkv, wf, b_f[None], jnp.tile(k_norm_g, N_HEADS)[None], gg,
                                        tt=tt_proj)
    c_main, c_meta = _cumsum(lf[:r_main], lf[r_main:r_main + r_meta], n_batch=n_batch, seq=seq, tc=tc)
    n_qt = seq // tq
    cq_main = jnp.transpose(c_main.reshape(r_main, N_HEADS // 2, 2), (1, 0, 2))
    ck_main = jnp.transpose(c_main.reshape(n_batch * n_qt, tq, N_HEADS // 2, 2), (2, 0, 3, 1))
    cq_meta = jnp.transpose(c_meta.reshape(n_batch, N_META, N_HEADS // 2, 2), (0, 2, 1, 3))
    ck_meta = jnp.transpose(c_meta.reshape(n_batch, N_META, N_HEADS // 2, 2), (0, 2, 3, 1))

    kc = jnp.transpose(cache_k, (0, 2, 3, 1)).reshape(n_phys, d, page)
    vc = jnp.transpose(cache_v, (0, 2, 3, 1)).reshape(n_phys, d, page)
    lc = jnp.transpose(cache_logf, (0, 2, 1))

    def to_seq_major(rows):
        return jnp.transpose(rows.reshape(n_new, n_seq, rows.shape[-1]), (1, 0, 2))

    s0 = r_main + r_meta
    k_s, v_s, lf_s = to_seq_major(k_all[s0:]), to_seq_major(v_all[s0:]), to_seq_major(lf[s0:])
    lf_s_t = jnp.transpose(lf_s, (0, 2, 1))

    for layer in range(n_a, depth):
        jb = layer - n_a
        q = _q_proj(h, attn_norm_g[jb][None], w_q[jb].astype(BF16), jnp.tile(q_norm_g[jb], N_HEADS)[None], gg, tt=tt_proj)
        a_main, a_meta = _prompt_attn(q, kb, vb, cq_main, cq_meta, ck_main, ck_meta, n_batch=n_batch, seq=seq, tq=tq)
        a_s = _decode_attn(page_table, to_seq_major(q[s0:]).astype(F32), kc, vc, lc, k_s, v_s, lf_s_t)
        a = jnp.concatenate([a_main, a_meta, jnp.transpose(a_s, (1, 0, 2)).reshape(r_s, d).astype(BF16)], axis=0)
        h = _o_proj(h, a, w_o[jb].astype(BF16), tt=tt_proj)
        h = moe(h, layer)

    def with_meta(main_rows, meta_rows):
        c = main_rows.shape[-1]
        return jnp.concatenate([meta_rows.reshape(n_batch, N_META, c), main_rows.reshape(n_batch, seq, c)], axis=1)

    y_prompt = h[:r_main].reshape(n_batch, seq, d)
    y_sample = to_seq_major(h[s0:])
    pool_state_prompt = jnp.transpose(jnp.stack(tails_p, axis=0), (0, 2, 1, 3))
    pool_state_sample = jnp.transpose(jnp.stack(tails_s, axis=0), (0, 2, 1, 3))
    k_p = with_meta(k_all[:r_main], k_all[r_main:s0]).reshape(n_batch, seq + N_META, N_HEADS, HEAD_DIM)
    v_p = with_meta(v_all[:r_main], v_all[r_main:s0]).reshape(n_batch, seq + N_META, N_HEADS, HEAD_DIM)
    logf_p = with_meta(lf[:r_main], lf[r_main:s0])
    return (y_prompt, y_sample, pool_state_prompt, pool_state_sample, k_p, v_p, logf_p,
            k_s.reshape(n_seq, n_new, N_HEADS, HEAD_DIM), v_s.reshape(n_seq, n_new, N_HEADS, HEAD_DIM), lf_s)
```

```python
import functools

import jax
import jax.numpy as jnp
from jax import lax
from jax.experimental import pallas as pl
from jax.experimental.pallas import tpu as pltpu

F32 = jnp.float32
BF16 = jnp.bfloat16
I32 = jnp.int32

N_META = 16
POOL_WINDOWS = (2, 4, 8, 16)
POOL_STATE = max(POOL_WINDOWS) - 1
N_HEADS = 16
HEAD_DIM = 64
N_GROUPS = 4
EXPERTS_PER_GROUP = 4
N_EXPERTS = N_GROUPS * EXPERTS_PER_GROUP
ATTN_SCALE = HEAD_DIM ** -0.5
EPS = 1e-6
NEG_INF = -1e30

LANES = 128
SUBLANES = 8
MXU_DIM = 256
VMEM_LIMIT_BYTES = 56 * 1024 * 1024


def _params(sem, vmem=None):
    return pltpu.CompilerParams(dimension_semantics=sem, vmem_limit_bytes=vmem)


def _rms(x, g):
    return x * lax.rsqrt(jnp.mean(x * x, axis=-1, keepdims=True) + EPS) * g


def _dot(a, b):
    return jnp.dot(a, b, preferred_element_type=F32)


def _dot_nt(a, b):
    return lax.dot_general(a, b, (((1,), (1,)), ((), ())), preferred_element_type=F32)


def _split2(a):
    hi = a.astype(BF16)
    lo = (a - hi.astype(F32)).astype(BF16)
    return hi, lo


def _split3(a):
    hi = a.astype(BF16)
    r = a - hi.astype(F32)
    mid = r.astype(BF16)
    lo = (r - mid.astype(F32)).astype(BF16)
    return hi, mid, lo


def _dot3(a, b):
    ah, al = _split2(a)
    bh, bl = _split2(b)
    return _dot(ah, bh) + (_dot(ah, bl) + _dot(al, bh))


def _dot_exact_rhs(mask_bf16, b):
    b0, b1, b2 = _split3(b)
    return _dot(mask_bf16, b0) + (_dot(mask_bf16, b1) + _dot(mask_bf16, b2))


def _head_norm(k, gg, gain):
    ksq = k * k
    hi, lo = _split2(ksq)
    parts = []
    for c in range(k.shape[1] // MXU_DIM):
        sl = slice(c * MXU_DIM, (c + 1) * MXU_DIM)
        parts.append(_dot(hi[:, sl], gg) + _dot(lo[:, sl], gg))
    ms = jnp.concatenate(parts, axis=1) * (1.0 / HEAD_DIM)
    return k * lax.rsqrt(ms + EPS) * gain


def _pool_main_kernel(h_ref, hm_ref, g_ref, w_ref, sc_ref, o_ref, tail_ref, ext_ref, *, tt, n_t):
    i = pl.program_id(1)
    g = g_ref[...]
    gd = h_ref.shape[1] // len(POOL_WINDOWS)

    @pl.when(i == 0)
    def _():
        ext_ref[0:N_META, :] = _rms(hm_ref[...], g)

    x = h_ref[...]
    u = _rms(x, g)
    ext_ref[N_META:N_META + tt, :] = u
    outs = []
    for gi, w in enumerate(POOL_WINDOWS):
        lo, hi = gi * gd, (gi + 1) * gd
        ug = u[:, lo:hi]
        acc = ug
        for k in range(1, w):
            acc = acc + ext_ref[N_META - k:N_META - k + tt, lo:hi]
        diff = acc * (1.0 / w) - ug
        outs.append(_dot3(diff, w_ref[gi]))
    o_ref[...] = x + jnp.concatenate(outs, axis=1) * sc_ref[...]
    ext_ref[0:N_META, :] = ext_ref[tt:tt + N_META, :]

    @pl.when(i == n_t - 1)
    def _():
        tail_ref[0] = u[tt - N_META:tt, :]


def _pool_main(h, h_meta, g, w, sc, *, n_batch, seq, tt):
    t_rows, d = h.shape
    n_t = seq // tt
    kern = functools.partial(_pool_main_kernel, tt=tt, n_t=n_t)
    return pl.pallas_call(
        kern,
        grid=(n_batch, n_t),
        in_specs=[
            pl.BlockSpec((tt, d), lambda b, i: (b * n_t + i, 0)),
            pl.BlockSpec((N_META, d), lambda b, i: (b, 0)),
            pl.BlockSpec((1, d), lambda b, i: (0, 0)),
            pl.BlockSpec(w.shape, lambda b, i: (0, 0, 0)),
            pl.BlockSpec((1, d), lambda b, i: (0, 0)),
        ],
        out_specs=[
            pl.BlockSpec((tt, d), lambda b, i: (b * n_t + i, 0)),
            pl.BlockSpec((1, N_META, d), lambda b, i: (b, 0, 0)),
        ],
        out_shape=[jax.ShapeDtypeStruct(h.shape, F32), jax.ShapeDtypeStruct((n_batch, N_META, d), F32)],
        scratch_shapes=[pltpu.VMEM((N_META + tt, d), F32)],
        input_output_aliases={0: 0},
        compiler_params=_params(("arbitrary", "arbitrary"), VMEM_LIMIT_BYTES),
        name="pool_main",
    )(h, h_meta, g, w, sc)


def _pool_small_kernel(x_ref, prev_ref, g_ref, w_ref, sc_ref, y_ref, u_ref, *, n_new, pos0):
    g = g_ref[...]
    n_seq, d = x_ref.shape[1], x_ref.shape[2]
    gd = d // len(POOL_WINDOWS)
    us = [_rms(x_ref[t], g) for t in range(n_new)]
    ext = [prev_ref[j] for j in range(POOL_STATE)] + us
    outs = []
    for gi, w in enumerate(POOL_WINDOWS):
        lo, hi = gi * gd, (gi + 1) * gd
        diffs = []
        for t in range(n_new):
            ug = us[t][:, lo:hi]
            acc = ug
            for k in range(1, w):
                acc = acc + ext[POOL_STATE + t - k][:, lo:hi]
            cnt = float(min(pos0 + t + 1, w))
            diffs.append(acc / cnt - ug)
        outs.append(_dot3(jnp.concatenate(diffs, axis=0), w_ref[gi]))
    out = jnp.concatenate(outs, axis=1) * sc_ref[...]
    for t in range(n_new):
        y_ref[t] = x_ref[t] + out[t * n_seq:(t + 1) * n_seq, :]
        u_ref[t] = us[t]


def _pool_small(x, prev, g, w, sc, *, pos0):
    n_new = x.shape[0]
    kern = functools.partial(_pool_small_kernel, n_new=n_new, pos0=pos0)
    return pl.pallas_call(
        kern,
        out_shape=[jax.ShapeDtypeStruct(x.shape, F32), jax.ShapeDtypeStruct(x.shape, F32)],
        compiler_params=_params(None, VMEM_LIMIT_BYTES),
        name="pool_small",
    )(x, prev, g, w, sc)


def _moe_kernel(h_ref, g_ref, wr_ref, wg_ref, wu_ref, wd_ref, o_ref,
                u2d, slots, x_t, o_t, x_t2, o_t2, rt_i, gates, cnt_v, rt_si, cnt_s, lst_src, lst_dst,
                *, blk, tm, pitch, cap):
    e = pl.program_id(1)
    n_chunks = blk // LANES
    d = h_ref.shape[1]
    n_slab = d // LANES

    @pl.when(e == 0)
    def _router():
        g = g_ref[...]
        wr_hi, wr_lo = _split2(wr_ref[...])
        sub16 = lax.broadcasted_iota(I32, (N_EXPERTS, LANES), 0)
        tri = (lax.broadcasted_iota(I32, (LANES, LANES), 0)
               < lax.broadcasted_iota(I32, (LANES, LANES), 1)).astype(BF16)

        def chunk(c, carry):
            r0 = pl.multiple_of(c * LANES, LANES)
            u = _rms(h_ref[pl.ds(r0, LANES), :], g)
            for j in range(n_slab):
                u2d[pl.ds(c * (LANES * n_slab) + j, LANES, stride=n_slab), :] = u[:, j * LANES:(j + 1) * LANES]
            u_hi, u_lo = _split2(u)
            lt = _dot_nt(wr_hi, u_hi) + (_dot_nt(wr_hi, u_lo) + _dot_nt(wr_lo, u_hi))
            gl = [lt[k:k + 1, :] for k in range(N_GROUPS)]
            gmax = jnp.maximum(jnp.maximum(gl[0], gl[1]), jnp.maximum(gl[2], gl[3]))
            gsel = jnp.where(gl[0] >= gmax, 0, jnp.where(gl[1] >= gmax, 1, jnp.where(gl[2] >= gmax, 2, 3)))
            denom = (jnp.exp(gl[0] - gmax) + jnp.exp(gl[1] - gmax)) + (jnp.exp(gl[2] - gmax) + jnp.exp(gl[3] - gmax))
            p_top = 1.0 / denom
            le = []
            for k in range(EXPERTS_PER_GROUP):
                rows = [lt[N_GROUPS + gi * EXPERTS_PER_GROUP + k:N_GROUPS + gi * EXPERTS_PER_GROUP + k + 1, :]
                        for gi in range(N_GROUPS)]
                le.append(jnp.where(gsel == 0, rows[0], jnp.where(gsel == 1, rows[1],
                                                                   jnp.where(gsel == 2, rows[2], rows[3]))))
            v1 = jnp.maximum(jnp.maximum(le[0], le[1]), jnp.maximum(le[2], le[3]))
            i1 = jnp.where(le[0] >= v1, 0, jnp.where(le[1] >= v1, 1, jnp.where(le[2] >= v1, 2, 3)))
            rest = [jnp.where(i1 == k, NEG_INF, le[k]) for k in range(EXPERTS_PER_GROUP)]
            v2 = jnp.maximum(jnp.maximum(rest[0], rest[1]), jnp.maximum(rest[2], rest[3]))
            i2 = jnp.where((rest[0] >= v2) & (i1 != 0), 0,
                           jnp.where((rest[1] >= v2) & (i1 != 1), 1,
                                     jnp.where((rest[2] >= v2) & (i1 != 2), 2, 3)))
            ex = jnp.exp(v2 - v1)
            gate1 = p_top / (1.0 + ex)
            gate2 = p_top * ex / (1.0 + ex)
            e1 = gsel * EXPERTS_PER_GROUP + i1
            e2 = gsel * EXPERTS_PER_GROUP + i2
            oh1 = (sub16 == e1).astype(F32)
            oh2 = (sub16 == e2).astype(F32)
            both = oh1 + oh2
            rank = _dot(both.astype(BF16), tri) + carry
            pos1 = jnp.sum(oh1 * rank, axis=0, keepdims=True).astype(I32)
            pos2 = jnp.sum(oh2 * rank, axis=0, keepdims=True).astype(I32)
            rt_i[c] = jnp.zeros((SUBLANES, LANES), I32)
            gates[c] = jnp.zeros((SUBLANES, LANES), F32)
            rt_i[c, 0:1, :] = e1 * cap + pos1
            rt_i[c, 1:2, :] = e2 * cap + pos2
            gates[c, 0:1, :] = gate1
            gates[c, 1:2, :] = gate2
            return carry + jnp.sum(both, axis=1, keepdims=True)

        total = lax.fori_loop(0, n_chunks, chunk, jnp.zeros((N_EXPERTS, 1), F32))
        cnt_v[...] = jnp.broadcast_to(total, (N_EXPERTS, LANES)).astype(I32)
        pltpu.sync_copy(rt_i, rt_si)
        pltpu.sync_copy(cnt_v, cnt_s)

        def invert(c, _):
            for l in range(LANES):
                row = (c * LANES + l) * n_slab
                a1 = rt_si[c, 0, l]
                a2 = rt_si[c, 1, l]
                lst_src[a1] = row
                lst_src[a2] = row
                lst_dst[a1] = row
                lst_dst[a2] = row + blk * n_slab
            return 0

        lax.fori_loop(0, n_chunks, invert, 0)

        for ex in range(N_EXPERTS):
            n = cnt_s[ex, 0]
            n_pad = ((n + (tm - 1)) // tm) * tm

            def pad(i, _, ex=ex, n=n):
                lst_src[ex * cap + i] = lst_src[ex * cap + n - 1]
                lst_dst[ex * cap + i] = lst_dst[ex * cap + n - 1]
                return 0

            lax.fori_loop(n, n_pad, pad, 0)

    n_e = cnt_s[e, 0]
    n_tiles = (n_e + (tm - 1)) // tm

    def tiles(first_tile, bufs):
        bases = [e * cap + (first_tile + t) * tm for t in range(len(bufs))]
        for base, (xb, _) in zip(bases, bufs):
            for r in range(tm):
                src = pl.multiple_of(lst_src[base + r], n_slab)
                xb[pl.ds(r, n_slab, stride=pitch), :] = u2d[pl.ds(src, n_slab), :]
        for base, (xb, ob) in zip(bases, bufs):
            x = jnp.concatenate([xb[j * pitch:j * pitch + tm, :] for j in range(n_slab)], axis=1).astype(BF16)
            hg = _dot(x, wg_ref[0, 0])
            hu = _dot(x, wu_ref[0, 0])
            act = (hg * jax.nn.sigmoid(hg) * hu).astype(BF16)
            o = _dot(act, wd_ref[0, 0])
            for j in range(n_slab):
                ob[j * pitch:j * pitch + tm, :] = o[:, j * LANES:(j + 1) * LANES]
            for r in range(tm):
                dst = pl.multiple_of(lst_dst[base + r], n_slab)
                slots[pl.ds(dst, n_slab), :] = ob[pl.ds(r, n_slab, stride=pitch), :]

    def pair(pi, _):
        tiles(2 * pi, [(x_t, o_t), (x_t2, o_t2)])
        return 0

    lax.fori_loop(0, n_tiles // 2, pair, 0)

    @pl.when(n_tiles % 2 == 1)
    def _():
        tiles(n_tiles - 1, [(x_t, o_t)])

    @pl.when(e == N_EXPERTS - 1)
    def _combine():
        eye = (lax.broadcasted_iota(I32, (LANES, LANES), 0)
               == lax.broadcasted_iota(I32, (LANES, LANES), 1)).astype(BF16)

        def chunk(c, _):
            r0 = pl.multiple_of(c * LANES, LANES)
            g0, g1, g2 = _split3(gates[c])
            gcol = _dot_nt(eye, g0) + (_dot_nt(eye, g1) + _dot_nt(eye, g2))
            ga, gb = gcol[:, 0:1], gcol[:, 1:2]
            for j in range(n_slab):
                a = slots[pl.ds(c * (LANES * n_slab) + j, LANES, stride=n_slab), :]
                b = slots[pl.ds((blk + c * LANES) * n_slab + j, LANES, stride=n_slab), :]
                o_ref[pl.ds(r0, LANES), j * LANES:(j + 1) * LANES] = (
                    h_ref[pl.ds(r0, LANES), j * LANES:(j + 1) * LANES] + (ga * a + gb * b))
            return 0

        lax.fori_loop(0, n_chunks, chunk, 0)


def _moe(h, g, wr_t, wg, wu, wd, *, layer, blk, tm):
    t_rows, d = h.shape
    f = wg.shape[3]
    n_blocks = t_rows // blk
    n_chunks = blk // LANES
    n_slab = d // LANES
    pitch = tm + SUBLANES
    cap = -(-blk // tm) * tm
    kern = functools.partial(_moe_kernel, blk=blk, tm=tm, pitch=pitch, cap=cap)
    return pl.pallas_call(
        kern,
        grid=(n_blocks, N_EXPERTS),
        in_specs=[
            pl.BlockSpec((blk, d), lambda i, e: (i, 0)),
            pl.BlockSpec((1, d), lambda i, e: (0, 0)),
            pl.BlockSpec(wr_t.shape, lambda i, e: (0, 0)),
            pl.BlockSpec((1, 1, d, f), lambda i, e: (layer, e, 0, 0)),
            pl.BlockSpec((1, 1, d, f), lambda i, e: (layer, e, 0, 0)),
            pl.BlockSpec((1, 1, f, d), lambda i, e: (layer, e, 0, 0)),
        ],
        out_specs=pl.BlockSpec((blk, d), lambda i, e: (i, 0), pipeline_mode=pl.Buffered(1)),
        out_shape=jax.ShapeDtypeStruct(h.shape, F32),
        scratch_shapes=[
            pltpu.VMEM((blk * n_slab, LANES), F32),
            pltpu.VMEM((2 * blk * n_slab, LANES), F32),
            pltpu.VMEM((n_slab * pitch, LANES), F32),
            pltpu.VMEM((n_slab * pitch, LANES), F32),
            pltpu.VMEM((n_slab * pitch, LANES), F32),
            pltpu.VMEM((n_slab * pitch, LANES), F32),
            pltpu.VMEM((n_chunks, SUBLANES, LANES), I32),
            pltpu.VMEM((n_chunks, SUBLANES, LANES), F32),
            pltpu.VMEM((N_EXPERTS, LANES), I32),
            pltpu.SMEM((n_chunks, SUBLANES, LANES), I32),
            pltpu.SMEM((N_EXPERTS, LANES), I32),
            pltpu.SMEM((N_EXPERTS * cap,), I32),
            pltpu.SMEM((N_EXPERTS * cap,), I32),
        ],
        input_output_aliases={0: 0},
        compiler_params=_params(("arbitrary", "arbitrary"), VMEM_LIMIT_BYTES),
        name="moe",
    )(h, g, wr_t, wg, wu, wd)


def _kv_kernel(h_ref, g_ref, wkv_ref, wf_ref, bf_ref, kg_ref, gg_ref, k_ref, v_ref, kb_ref, vb_ref, lf_ref):
    d = h_ref.shape[1]
    u = _rms(h_ref[...], g_ref[...])
    p = _dot(u.astype(BF16), wkv_ref[...])
    kn = _head_norm(p[:, :d], gg_ref[...], kg_ref[...])
    v = p[:, d:]
    k_ref[...] = kn
    v_ref[...] = v
    kb_ref[...] = kn.astype(BF16)
    vb_ref[...] = v.astype(BF16)
    z = _dot3(u, wf_ref[...])[:, :N_HEADS] + bf_ref[...]
    lf_ref[...] = jnp.minimum(z, 0.0) - jnp.log1p(jnp.exp(-jnp.abs(z)))


def _kv_proj(h, g, wkv, wf, bfr, kg, gg, *, tt):
    t_rows, d = h.shape
    row = lambda i: (i, 0)
    const = lambda i: (0, 0)
    return pl.pallas_call(
        _kv_kernel,
        grid=(t_rows // tt,),
        in_specs=[
            pl.BlockSpec((tt, d), row),
            pl.BlockSpec((1, d), const),
            pl.BlockSpec(wkv.shape, const),
            pl.BlockSpec(wf.shape, const),
            pl.BlockSpec((1, N_HEADS), const),
            pl.BlockSpec((1, d), const),
            pl.BlockSpec(gg.shape, const),
        ],
        out_specs=[pl.BlockSpec((tt, d), row)] * 4 + [pl.BlockSpec((tt, N_HEADS), row)],
        out_shape=[jax.ShapeDtypeStruct((t_rows, d), F32), jax.ShapeDtypeStruct((t_rows, d), F32),
                   jax.ShapeDtypeStruct((t_rows, d), BF16), jax.ShapeDtypeStruct((t_rows, d), BF16),
                   jax.ShapeDtypeStruct((t_rows, N_HEADS), F32)],
        compiler_params=_params(("arbitrary",), VMEM_LIMIT_BYTES),
        name="kv_proj",
    )(h, g, wkv, wf, bfr, kg, gg)


def _q_kernel(h_ref, g_ref, wq_ref, qg_ref, gg_ref, q_ref):
    u = _rms(h_ref[...], g_ref[...])
    q = _dot(u.astype(BF16), wq_ref[...])
    q_ref[...] = (_head_norm(q, gg_ref[...], qg_ref[...]) * ATTN_SCALE).astype(BF16)


def _q_proj(h, g, wq, qg, gg, *, tt):
    t_rows, d = h.shape
    row = lambda i: (i, 0)
    const = lambda i: (0, 0)
    return pl.pallas_call(
        _q_kernel,
        grid=(t_rows // tt,),
        in_specs=[pl.BlockSpec((tt, d), row), pl.BlockSpec((1, d), const), pl.BlockSpec(wq.shape, const),
                  pl.BlockSpec((1, d), const), pl.BlockSpec(gg.shape, const)],
        out_specs=pl.BlockSpec((tt, d), row),
        out_shape=jax.ShapeDtypeStruct((t_rows, d), BF16),
        compiler_params=_params(("arbitrary",), VMEM_LIMIT_BYTES),
        name="q_proj",
    )(h, g, wq, qg, gg)


def _o_kernel(h_ref, am_ref, at_ref, wo_ref, o_ref, *, n_main):
    a = jnp.where(pl.program_id(0) < n_main, am_ref[...], at_ref[...])
    o_ref[...] = h_ref[...] + _dot(a, wo_ref[...])


def _o_proj(h, a_main, a_tail, wo, *, tt):
    t_rows, d = h.shape
    n_main = a_main.shape[0] // tt
    row = lambda i: (i, 0)
    return pl.pallas_call(
        functools.partial(_o_kernel, n_main=n_main),
        grid=(t_rows // tt,),
        in_specs=[pl.BlockSpec((tt, d), row),
                  pl.BlockSpec((tt, d), lambda i: (jnp.minimum(i, n_main - 1), 0)),
                  pl.BlockSpec((tt, d), lambda i: (jnp.maximum(i - n_main, 0), 0)),
                  pl.BlockSpec(wo.shape, lambda i: (0, 0))],
        out_specs=pl.BlockSpec((tt, d), row),
        out_shape=jax.ShapeDtypeStruct(h.shape, F32),
        input_output_aliases={0: 0},
        compiler_params=_params(("arbitrary",), VMEM_LIMIT_BYTES),
        name="o_proj",
    )(h, a_main, a_tail, wo)


def _bias_placement(d):
    h = jnp.arange(N_HEADS)
    base = (h // 2) * LANES + 6 * (h % 2)
    col = jnp.arange(d)[None, None, :]
    i = jnp.arange(3)[:, None, None]
    hit_q = col == (base[None, :, None] + i)
    hit_k = col == (base[None, :, None] + i + 3)
    place_q = hit_q.astype(BF16)
    place_k = -hit_k.astype(BF16)
    ones_q = jnp.any(hit_k, axis=(0, 1)).astype(F32)[None]
    ones_k = jnp.any(hit_q, axis=(0, 1)).astype(F32)[None]
    return place_q, place_k, ones_q, ones_k


def _cumsum_kernel(lm_ref, lt_ref, pq_ref, pk_ref, oq_ref, ok_ref, bqm_ref, bkm_ref, bqt_ref, bkt_ref, ct_ref,
                   *, seq, tc):
    def emit(c, bq_out, bk_out):
        pieces = _split3(c)
        bq = _dot(pieces[0], pq_ref[0]) + (_dot(pieces[1], pq_ref[1]) + _dot(pieces[2], pq_ref[2])) + oq_ref[...]
        bk = _dot(pieces[0], pk_ref[0]) + (_dot(pieces[1], pk_ref[1]) + _dot(pieces[2], pk_ref[2])) + ok_ref[...]
        bq_out(bq.astype(BF16))
        bk_out(bk.astype(BF16))

    cur = jnp.zeros((1, N_HEADS), F32)
    for r in range(N_META):
        cur = cur + lt_ref[r:r + 1, :]
        ct_ref[r:r + 1, :] = cur

    def set_meta_q(x):
        bqt_ref[...] = x

    def set_meta_k(x):
        bkt_ref[...] = x

    emit(ct_ref[...], set_meta_q, set_meta_k)
    tril = (lax.broadcasted_iota(I32, (tc, tc), 1) <= lax.broadcasted_iota(I32, (tc, tc), 0)).astype(BF16)
    for c in range(seq // tc):
        rows = slice(c * tc, (c + 1) * tc)
        blk = _dot_exact_rhs(tril, lm_ref[rows, :]) + cur

        def set_q(x, rows=rows):
            bqm_ref[rows, :] = x

        def set_k(x, rows=rows):
            bkm_ref[rows, :] = x

        emit(blk, set_q, set_k)
        cur = blk[tc - 1:tc, :]


def _cumsum(lf_main, lf_meta, d, *, n_batch, seq, tc):
    kern = functools.partial(_cumsum_kernel, seq=seq, tc=tc)
    place_q, place_k, ones_q, ones_k = _bias_placement(d)
    const3 = lambda b: (0, 0, 0)
    const2 = lambda b: (0, 0)
    return pl.pallas_call(
        kern,
        grid=(n_batch,),
        in_specs=[pl.BlockSpec((seq, N_HEADS), lambda b: (b, 0)), pl.BlockSpec((N_META, N_HEADS), lambda b: (b, 0)),
                  pl.BlockSpec(place_q.shape, const3), pl.BlockSpec(place_k.shape, const3),
                  pl.BlockSpec(ones_q.shape, const2), pl.BlockSpec(ones_k.shape, const2)],
        out_specs=[pl.BlockSpec((seq, d), lambda b: (b, 0)), pl.BlockSpec((seq, d), lambda b: (b, 0)),
                   pl.BlockSpec((N_META, d), lambda b: (b, 0)), pl.BlockSpec((N_META, d), lambda b: (b, 0))],
        out_shape=[jax.ShapeDtypeStruct((n_batch * seq, d), BF16), jax.ShapeDtypeStruct((n_batch * seq, d), BF16),
                   jax.ShapeDtypeStruct((n_batch * N_META, d), BF16), jax.ShapeDtypeStruct((n_batch * N_META, d), BF16)],
        scratch_shapes=[pltpu.VMEM((N_META, N_HEADS), F32)],
        compiler_params=_params(("arbitrary",), VMEM_LIMIT_BYTES),
        name="logf_cumsum",
    )(lf_main, lf_meta, place_q, place_k, ones_q, ones_k)


def _attend(qa, ka, vx, mask, m, acc):
    s = _dot_nt(qa, ka)
    if mask is not None:
        s = jnp.where(mask, s, NEG_INF)
    m_new = jnp.maximum(m, jnp.max(s, axis=1, keepdims=True))
    alpha = jnp.exp(m - m_new)
    p = jnp.exp(s - m_new).astype(BF16)
    return m_new, alpha * acc + _dot(p, vx)


def _attn_kernel(qm_ref, qt_ref, km_ref, kt_ref, vm_ref, vt_ref, bqm_ref, bqt_ref, bkm_ref, bkt_ref,
                 om_ref, ot_ref, *, seq, tq):
    lane = lax.broadcasted_iota(I32, (1, LANES), 1)
    first = lane < HEAD_DIM
    one = jnp.ones((), BF16)

    def q_heads(q2, bq):
        keep = [lane < 6, (lane >= 6) & (lane < 12)]
        return [jnp.concatenate([jnp.where(first, q2, 0) if hd == 0 else jnp.where(first, 0, q2),
                                 jnp.where(keep[hd], bq, 0)], axis=1) for hd in range(2)]

    def v_heads(v2):
        return [jnp.where(first, v2, one), jnp.where(first, one, v2)]

    def finish(accs):
        outs = [acc / pltpu.roll(acc, HEAD_DIM, axis=1) for acc in accs]
        return jnp.where(first, outs[0], outs[1]).astype(BF16)

    def init(rows):
        return jnp.full((rows, 1), NEG_INF, F32), jnp.zeros((rows, LANES), F32)

    def both(qa, ka, vx, mask, state):
        a = _attend(qa[0], ka, vx[0], mask, state[0], state[1])
        b = _attend(qa[1], ka, vx[1], mask, state[2], state[3])
        return a + b

    ka_t = jnp.concatenate([kt_ref[...], bkt_ref[...]], axis=1)
    vx_t = v_heads(vt_ref[...])

    causal_t = (lax.broadcasted_iota(I32, (N_META, N_META), 1) <= lax.broadcasted_iota(I32, (N_META, N_META), 0))
    st = both(q_heads(qt_ref[...], bqt_ref[...]), ka_t, vx_t, causal_t, init(N_META) + init(N_META))
    ot_ref[...] = finish([st[1], st[3]])

    causal = (lax.broadcasted_iota(I32, (tq, tq), 1) <= lax.broadcasted_iota(I32, (tq, tq), 0))
    for qi in range(seq // tq):
        r0 = qi * tq
        qa = q_heads(qm_ref[r0:r0 + tq, :], bqm_ref[r0:r0 + tq, :])
        state = both(qa, ka_t, vx_t, None, init(tq) + init(tq))

        def body(j, st, qa=qa):
            c0 = pl.multiple_of(j * tq, tq)
            ka = jnp.concatenate([km_ref[pl.ds(c0, tq), :], bkm_ref[pl.ds(c0, tq), :]], axis=1)
            return both(qa, ka, v_heads(vm_ref[pl.ds(c0, tq), :]), None, st)

        state = lax.fori_loop(0, qi, body, state)
        ka = jnp.concatenate([km_ref[r0:r0 + tq, :], bkm_ref[r0:r0 + tq, :]], axis=1)
        state = both(qa, ka, v_heads(vm_ref[r0:r0 + tq, :]), causal, state)
        om_ref[r0:r0 + tq, :] = finish([state[1], state[3]])


def _prompt_attn(q, k, v, bq_main, bq_meta, bk_main, bk_meta, *, n_batch, seq, tq):
    t_rows, d = q.shape
    n_pairs = d // LANES
    meta0 = (n_batch * seq) // N_META
    main = pl.BlockSpec((seq, LANES), lambda b, p: (b, p))
    meta = pl.BlockSpec((N_META, LANES), lambda b, p: (meta0 + b, p))
    small = pl.BlockSpec((N_META, LANES), lambda b, p: (b, p))
    kern = functools.partial(_attn_kernel, seq=seq, tq=tq)
    return pl.pallas_call(
        kern,
        grid=(n_batch, n_pairs),
        in_specs=[main, meta, main, meta, main, meta, main, small, main, small],
        out_specs=[main, small],
        out_shape=[jax.ShapeDtypeStruct((n_batch * seq, d), BF16), jax.ShapeDtypeStruct((n_batch * N_META, d), BF16)],
        compiler_params=_params(("arbitrary", "arbitrary"), VMEM_LIMIT_BYTES),
        name="prompt_attn",
    )(q, q, k, k, v, v, bq_main, bq_meta, bk_main, bk_meta)


def _decode_kernel(pt_ref, q_ref, *refs, n_steps, n_new, ppg):
    kc_refs, vc_refs, lc_refs = refs[:ppg], refs[ppg:2 * ppg], refs[2 * ppg:3 * ppg]
    kn_ref, vn_ref, ln_ref, o_ref, qbd, qbd_b, csq, m_s, l_s, acc_s, run_s = refs[3 * ppg:]
    j = pl.program_id(1)
    d = q_ref.shape[2]
    rows = n_new * N_HEADS
    head_of_col = lax.broadcasted_iota(I32, (N_HEADS, d), 1) // HEAD_DIM
    head_of_row = lax.broadcasted_iota(I32, (N_HEADS, d), 0)
    diag = head_of_col == head_of_row

    @pl.when(j == 0)
    def _():
        for qi in range(n_new):
            qbd[qi * N_HEADS:(qi + 1) * N_HEADS, :] = jnp.where(diag, q_ref[0, qi:qi + 1, :].astype(F32), 0.0)
        qbd_b[...] = qbd[...].astype(BF16)
        cur = jnp.zeros((N_HEADS, 1), F32)
        for qi in range(n_new):
            cur = cur + ln_ref[0, :, qi:qi + 1]
            csq[qi * N_HEADS:(qi + 1) * N_HEADS, :] = cur
        m_s[...] = jnp.full(m_s.shape, NEG_INF, F32)
        l_s[...] = jnp.zeros(l_s.shape, F32)
        acc_s[...] = jnp.zeros(acc_s.shape, F32)
        run_s[...] = jnp.zeros(run_s.shape, F32)

    page = lc_refs[0].shape[2]
    later = (lax.broadcasted_iota(I32, (page, page), 0) > lax.broadcasted_iota(I32, (page, page), 1)).astype(BF16)
    qb = qbd_b[...]
    run = run_s[...]
    scores = []
    for pg in range(ppg):
        lf = lc_refs[pg][0]
        l0, l1, l2 = _split3(lf)
        suffix = _dot(l0, later) + (_dot(l1, later) + _dot(l2, later)) + run
        run = run + jnp.sum(lf, axis=1, keepdims=True)
        scores.append(_dot(qb, kc_refs[pg][0].astype(BF16)) + jnp.concatenate([suffix] * n_new, axis=0))
    run_s[...] = run
    s = jnp.concatenate(scores, axis=1) + csq[...]
    m_new = jnp.maximum(m_s[...], jnp.max(s, axis=1, keepdims=True))
    alpha = jnp.exp(m_s[...] - m_new)
    p32 = jnp.exp(s - m_new)
    l_s[...] = alpha * l_s[...] + jnp.sum(p32, axis=1, keepdims=True)
    p = p32.astype(BF16)
    pv =_dot_nt(p[:, 0:page], vc_refs[0][0].astype(BF16))
    for pg in range(1, ppg):
        pv = pv + _dot_nt(p[:, pg * page:(pg + 1) * page], vc_refs[pg][0].astype(BF16))
    acc_s[...] = alpha * acc_s[...] + pv
    m_s[...] = m_new

    @pl.when(j == n_steps - 1)
    def _():
        row_q = lax.broadcasted_iota(I32, (rows, 1), 0) // N_HEADS
        m, l, acc = m_s[...], l_s[...], acc_s[...]
        qf = qbd[...]
        for jn in range(n_new):
            kn = kn_ref[0, jn:jn + 1, :]
            s = jnp.sum(qf * kn, axis=1, keepdims=True)
            cj = csq[jn * N_HEADS:(jn + 1) * N_HEADS, :]
            s = s + (csq[...] - jnp.concatenate([cj] * n_new, axis=0))
            s = jnp.where(row_q >= jn, s, NEG_INF)
            m_new = jnp.maximum(m, s)
            alpha = jnp.exp(m - m_new)
            p = jnp.exp(s - m_new)
            l = alpha * l + p
            acc = alpha * acc + p * vn_ref[0, jn:jn + 1, :]
            m = m_new
        out = acc / l
        for qi in range(n_new):
            blk = jnp.where(diag, out[qi * N_HEADS:(qi + 1) * N_HEADS, :], 0.0)
            o_ref[0, qi:qi + 1, :] = jnp.sum(blk, axis=0, keepdims=True).astype(o_ref.dtype)


def _decode_attn(page_table, q, kc, vc, lc, kn, vn, ln_t):
    n_seq, n_new, d = q.shape
    n_pages = page_table.shape[1]
    page = kc.shape[2]
    rows = n_new * N_HEADS
    ppg = _pick((n_pages,), (8, 4, 2, 1))
    n_steps = n_pages // ppg
    kern = functools.partial(_decode_kernel, n_steps=n_steps, n_new=n_new, ppg=ppg)
    seq3 = lambda n, j, pt: (n, 0, 0)

    def pg3(pg):
        return lambda n, j, pt: (pt[n, n_pages - 1 - (j * ppg + pg)], 0, 0)

    grid_spec = pltpu.PrefetchScalarGridSpec(
        num_scalar_prefetch=1,
        grid=(n_seq, n_steps),
        in_specs=([pl.BlockSpec((1, n_new, d), seq3)]
                  + [pl.BlockSpec((1, d, page), pg3(pg)) for pg in range(ppg)]
                  + [pl.BlockSpec((1, d, page), pg3(pg)) for pg in range(ppg)]
                  + [pl.BlockSpec((1, N_HEADS, page), pg3(pg)) for pg in range(ppg)]
                  + [pl.BlockSpec((1, n_new, d), seq3), pl.BlockSpec((1, n_new, d), seq3),
                     pl.BlockSpec((1, N_HEADS, n_new), seq3)]),
        out_specs=pl.BlockSpec((1, n_new, d), seq3),
        scratch_shapes=[pltpu.VMEM((rows, d), F32), pltpu.VMEM((rows, d), BF16), pltpu.VMEM((rows, 1), F32),
                        pltpu.VMEM((rows, 1), F32), pltpu.VMEM((rows, 1), F32), pltpu.VMEM((rows, d), F32),
                        pltpu.VMEM((N_HEADS, 1), F32)],
    )
    return pl.pallas_call(
        kern,
        grid_spec=grid_spec,
        out_shape=jax.ShapeDtypeStruct((n_seq, n_new, d), F32),
        compiler_params=_params(("arbitrary", "arbitrary"), VMEM_LIMIT_BYTES),
        name="decode_attn",
    )(page_table, q, *([kc] * ppg), *([vc] * ppg), *([lc] * ppg), kn, vn, ln_t)


def _pick(totals, cands):
    for c in cands:
        if all(t % c == 0 for t in totals):
            return c
    raise ValueError(f"no tile among {cands} divides {totals}")


def kernel(x_prompt, x_sample, state_pool, cache_k, cache_v, cache_logf, page_table, meta, pool_norm_g, pool_w,
           pool_scale, kv_norm_g, w_kvf, b_f, k_norm_g, attn_norm_g, w_q, q_norm_g, w_o, ffn_norm_g,
           w_router_group, w_router_expert, w_gate, w_up, w_down):
    n_batch, seq, d = x_prompt.shape
    n_seq, n_new, _ = x_sample.shape
    depth = ffn_norm_g.shape[0]
    n_a = pool_norm_g.shape[0]
    n_phys, page = cache_k.shape[0], cache_k.shape[1]
    past_len = page_table.shape[1] * page
    r_main, r_meta, r_s = n_batch * seq, n_batch * N_META, n_seq * n_new
    t_rows = r_main + r_meta + r_s
    assert d == N_HEADS * HEAD_DIM and d % (LANES * SUBLANES) == 0 and n_batch % SUBLANES == 0 and n_seq % SUBLANES == 0

    tt_pool = _pick((seq, t_rows), (512, 256, 128))
    tt_proj = _pick((t_rows,), (640, 512, 256, 128))
    tt_o = _pick((r_main, r_meta + r_s), (256, 128))
    blk_moe = _pick((t_rows,), (1664, 1280, 1024, 512, 256, 128))
    tq = _pick((seq,), (512, 256, 128))
    tc = _pick((seq,), (256, 128))

    xs_t = jnp.transpose(x_sample, (1, 0, 2))
    h = jnp.concatenate([x_prompt.reshape(r_main, d),
                         jnp.broadcast_to(meta[None], (n_batch, N_META, d)).reshape(r_meta, d),
                         xs_t.reshape(r_s, d)], axis=0)

    gg = (jnp.arange(MXU_DIM)[:, None] // HEAD_DIM == jnp.arange(MXU_DIM)[None, :] // HEAD_DIM).astype(BF16)
    state_t = jnp.transpose(state_pool, (0, 2, 1, 3))
    zero_prev = jnp.zeros((POOL_STATE, n_batch, d), F32)

    def put(hbuf, rows, start):
        return lax.dynamic_update_slice(hbuf, rows, (start, 0))

    def moe(hbuf, layer):
        wr = jnp.concatenate([w_router_group[layer].T,
                              jnp.transpose(w_router_expert[layer], (0, 2, 1)).reshape(N_EXPERTS, d),
                              jnp.zeros((32 - N_GROUPS - N_EXPERTS, d), F32)], axis=0)
        return _moe(hbuf, ffn_norm_g[layer][None], wr, wg_b, wu_b, wd_b, layer=layer, blk=blk_moe, tm=128)

    wg_b, wu_b, wd_b = w_gate.astype(BF16), w_up.astype(BF16), w_down.astype(BF16)

    tails_p, tails_s = [], []
    for layer in range(n_a):
        g = pool_norm_g[layer][None]
        sc = pool_scale[layer][None]
        w = pool_w[layer]
        h_meta = h[r_main:r_main + r_meta]
        h_s = h[r_main + r_meta:]
        h, tail = _pool_main(h, h_meta, g, w, sc, n_batch=n_batch, seq=seq, tt=tt_pool)
        meta_t = jnp.transpose(h_meta.reshape(n_batch, N_META, d), (1, 0, 2))
        y_meta, _ = _pool_small(meta_t, zero_prev, g, w, sc, pos0=0)
        y_s, u_s = _pool_small(h_s.reshape(n_new, n_seq, d), state_t[layer], g, w, sc, pos0=past_len)
        h = put(h, jnp.transpose(y_meta, (1, 0, 2)).reshape(r_meta, d), r_main)
        h = put(h, y_s.reshape(r_s, d), r_main + r_meta)
        tails_p.append(jnp.transpose(tail[:, 1:], (1, 0, 2)))
        tails_s.append(jnp.concatenate([state_t[layer][n_new:], u_s], axis=0))
        h = moe(h, layer)

    wkv = w_kvf[:, :2 * d].astype(BF16)
    wf = jnp.pad(w_kvf[:, 2 * d:], ((0, 0), (0, LANES - N_HEADS)))
    k_all, v_all, kb, vb, lf = _kv_proj(h, kv_norm_g[None], wkv, wf, b_f[None], jnp.tile(k_norm_g, N_HEADS)[None], gg,
                                        tt=tt_proj)
    bq_main, bk_main, bq_meta, bk_meta = _cumsum(lf[:r_main], lf[r_main:r_main + r_meta], d,
                                                 n_batch=n_batch, seq=seq, tc=tc)

    kc = jnp.transpose(cache_k, (0, 2, 3, 1)).reshape(n_phys, d, page)
    vc = jnp.transpose(cache_v, (0, 2, 3, 1)).reshape(n_phys, d, page)
    lc = jnp.transpose(cache_logf, (0, 2, 1))

    def to_seq_major(rows):
        return jnp.transpose(rows.reshape(n_new, n_seq, rows.shape[-1]), (1, 0, 2))

    s0 = r_main + r_meta
    k_s, v_s, lf_s = to_seq_major(k_all[s0:]), to_seq_major(v_all[s0:]), to_seq_major(lf[s0:])
    lf_s_t = jnp.transpose(lf_s, (0, 2, 1))

    for layer in range(n_a, depth):
        jb = layer - n_a
        q = _q_proj(h, attn_norm_g[jb][None], w_q[jb].astype(BF16), jnp.tile(q_norm_g[jb], N_HEADS)[None], gg, tt=tt_proj)
        a_main, a_meta = _prompt_attn(q, kb, vb, bq_main, bq_meta, bk_main, bk_meta, n_batch=n_batch, seq=seq, tq=tq)
        a_s = _decode_attn(page_table, to_seq_major(q[s0:]).astype(F32), kc, vc, lc, k_s, v_s, lf_s_t)
        a_tail = jnp.concatenate([a_meta, jnp.transpose(a_s, (1, 0, 2)).reshape(r_s, d).astype(BF16)], axis=0)
        h = _o_proj(h, a_main, a_tail, w_o[jb].astype(BF16), tt=tt_o)
        h = moe(h, layer)

    def with_meta(main_rows, meta_rows):
        c = main_rows.shape[-1]
        return jnp.concatenate([meta_rows.reshape(n_batch, N_META, c), main_rows.reshape(n_batch, seq, c)], axis=1)

    y_prompt = h[:r_main].reshape(n_batch, seq, d)
    y_sample = to_seq_major(h[s0:])
    pool_state_prompt = jnp.transpose(jnp.stack(tails_p, axis=0), (0, 2, 1, 3))
    pool_state_sample = jnp.transpose(jnp.stack(tails_s, axis=0), (0, 2, 1, 3))
    k_p = with_meta(k_all[:r_main], k_all[r_main:s0]).reshape(n_batch, seq + N_META, N_HEADS, HEAD_DIM)
    v_p = with_meta(v_all[:r_main], v_all[r_main:s0]).reshape(n_batch, seq + N_META, N_HEADS, HEAD_DIM)
    logf_p = with_meta(lf[:r_main], lf[r_main:s0])
    return (y_prompt, y_sample, pool_state_prompt, pool_state_sample, k_p, v_p, logf_p,
            k_s.reshape(n_seq, n_new, N_HEADS, HEAD_DIM), v_s.reshape(n_seq, n_new, N_HEADS, HEAD_DIM), lf_s)
```

```python
import functools

import jax
import jax.numpy as jnp
from jax import lax
from jax.experimental import pallas as pl
from jax.experimental.pallas import tpu as pltpu

F32 = jnp.float32
BF16 = jnp.bfloat16
I32 = jnp.int32

N_META = 16
POOL_WINDOWS = (2, 4, 8, 16)
POOL_STATE = max(POOL_WINDOWS) - 1
N_HEADS = 16
HEAD_DIM = 64
N_GROUPS = 4
EXPERTS_PER_GROUP = 4
N_EXPERTS = N_GROUPS * EXPERTS_PER_GROUP
ATTN_SCALE = HEAD_DIM ** -0.5
EPS = 1e-6
NEG_INF = -1e30

LANES = 128
SUBLANES = 8
MXU_DIM = 256
VMEM_LIMIT_BYTES = 56 * 1024 * 1024


def _params(sem, vmem=None):
    return pltpu.CompilerParams(dimension_semantics=sem, vmem_limit_bytes=vmem)


def _rms(x, g):
    return x * lax.rsqrt(jnp.mean(x * x, axis=-1, keepdims=True) + EPS) * g


def _dot(a, b):
    return jnp.dot(a, b, preferred_element_type=F32)


def _dot_nt(a, b):
    return lax.dot_general(a, b, (((1,), (1,)), ((), ())), preferred_element_type=F32)


def _split2(a):
    hi = a.astype(BF16)
    lo = (a - hi.astype(F32)).astype(BF16)
    return hi, lo


def _split3(a):
    hi = a.astype(BF16)
    r = a - hi.astype(F32)
    mid = r.astype(BF16)
    lo = (r - mid.astype(F32)).astype(BF16)
    return hi, mid, lo


def _dot3(a, b):
    ah, al = _split2(a)
    bh, bl = _split2(b)
    return _dot(ah, bh) + (_dot(ah, bl) + _dot(al, bh))


def _dot_exact_rhs(mask_bf16, b):
    b0, b1, b2 = _split3(b)
    return _dot(mask_bf16, b0) + (_dot(mask_bf16, b1) + _dot(mask_bf16, b2))


def _head_norm(k, gg, gain):
    ksq = k * k
    hi, lo = _split2(ksq)
    parts = []
    for c in range(k.shape[1] // MXU_DIM):
        sl = slice(c * MXU_DIM, (c + 1) * MXU_DIM)
        parts.append(_dot(hi[:, sl], gg) + _dot(lo[:, sl], gg))
    ms = jnp.concatenate(parts, axis=1) * (1.0 / HEAD_DIM)
    return k * lax.rsqrt(ms + EPS) * gain


def _pool_main_kernel(h_ref, hm_ref, g_ref, w_ref, sc_ref, o_ref, tail_ref, ext_ref, *, tt, n_t):
    i = pl.program_id(1)
    g = g_ref[...]
    gd = h_ref.shape[1] // len(POOL_WINDOWS)

    @pl.when(i == 0)
    def _():
        ext_ref[0:N_META, :] = _rms(hm_ref[...], g)

    x = h_ref[...]
    u = _rms(x, g)
    ext_ref[N_META:N_META + tt, :] = u
    outs = []
    for gi, w in enumerate(POOL_WINDOWS):
        lo, hi = gi * gd, (gi + 1) * gd
        ug = u[:, lo:hi]
        acc = ug
        for k in range(1, w):
            acc = acc + ext_ref[N_META - k:N_META - k + tt, lo:hi]
        diff = acc * (1.0 / w) - ug
        outs.append(_dot3(diff, w_ref[gi]))
    o_ref[...] = x + jnp.concatenate(outs, axis=1) * sc_ref[...]
    ext_ref[0:N_META, :] = ext_ref[tt:tt + N_META, :]

    @pl.when(i == n_t - 1)
    def _():
        tail_ref[0] = u[tt - N_META:tt, :]


def _pool_main(h, h_meta, g, w, sc, *, n_batch, seq, tt):
    t_rows, d = h.shape
    n_t = seq // tt
    kern = functools.partial(_pool_main_kernel, tt=tt, n_t=n_t)
    return pl.pallas_call(
        kern,
        grid=(n_batch, n_t),
        in_specs=[
            pl.BlockSpec((tt, d), lambda b, i: (b * n_t + i, 0)),
            pl.BlockSpec((N_META, d), lambda b, i: (b, 0)),
            pl.BlockSpec((1, d), lambda b, i: (0, 0)),
            pl.BlockSpec(w.shape, lambda b, i: (0, 0, 0)),
            pl.BlockSpec((1, d), lambda b, i: (0, 0)),
        ],
        out_specs=[
            pl.BlockSpec((tt, d), lambda b, i: (b * n_t + i, 0)),
            pl.BlockSpec((1, N_META, d), lambda b, i: (b, 0, 0)),
        ],
        out_shape=[jax.ShapeDtypeStruct(h.shape, F32), jax.ShapeDtypeStruct((n_batch, N_META, d), F32)],
        scratch_shapes=[pltpu.VMEM((N_META + tt, d), F32)],
        input_output_aliases={0: 0},
        compiler_params=_params(("arbitrary", "arbitrary"), VMEM_LIMIT_BYTES),
        name="pool_main",
    )(h, h_meta, g, w, sc)


def _pool_small_kernel(x_ref, prev_ref, g_ref, w_ref, sc_ref, y_ref, u_ref, *, n_new, pos0):
    g = g_ref[...]
    n_seq, d = x_ref.shape[1], x_ref.shape[2]
    gd = d // len(POOL_WINDOWS)
    us = [_rms(x_ref[t], g) for t in range(n_new)]
    ext = [prev_ref[j] for j in range(POOL_STATE)] + us
    outs = []
    for gi, w in enumerate(POOL_WINDOWS):
        lo, hi = gi * gd, (gi + 1) * gd
        diffs = []
        for t in range(n_new):
            ug = us[t][:, lo:hi]
            acc = ug
            for k in range(1, w):
                acc = acc + ext[POOL_STATE + t - k][:, lo:hi]
            cnt = float(min(pos0 + t + 1, w))
            diffs.append(acc / cnt - ug)
        outs.append(_dot3(jnp.concatenate(diffs, axis=0), w_ref[gi]))
    out = jnp.concatenate(outs, axis=1) * sc_ref[...]
    for t in range(n_new):
        y_ref[t] = x_ref[t] + out[t * n_seq:(t + 1) * n_seq, :]
        u_ref[t] = us[t]


def _pool_small(x, prev, g, w, sc, *, pos0):
    n_new = x.shape[0]
    kern = functools.partial(_pool_small_kernel, n_new=n_new, pos0=pos0)
    return pl.pallas_call(
        kern,
        out_shape=[jax.ShapeDtypeStruct(x.shape, F32), jax.ShapeDtypeStruct(x.shape, F32)],
        compiler_params=_params(None, VMEM_LIMIT_BYTES),
        name="pool_small",
    )(x, prev, g, w, sc)


def _moe_kernel(h_ref, g_ref, wr_ref, wg_ref, wu_ref, wd_ref, o_ref,
                u2d, slots, x_t, o_t, x_t2, o_t2, rt_i, gates, cnt_v, rt_si, cnt_s, lst_src, lst_dst,
                *, blk, tm, pitch, cap):
    e = pl.program_id(1)
    n_chunks = blk // LANES
    d = h_ref.shape[1]
    n_slab = d // LANES

    @pl.when(e == 0)
    def _router():
        g = g_ref[...]
        wr_hi, wr_lo = _split2(wr_ref[...])
        sub16 = lax.broadcasted_iota(I32, (N_EXPERTS, LANES), 0)
        tri = (lax.broadcasted_iota(I32, (LANES, LANES), 0)
               < lax.broadcasted_iota(I32, (LANES, LANES), 1)).astype(BF16)

        def chunk(c, carry):
            r0 = pl.multiple_of(c * LANES, LANES)
            u = _rms(h_ref[pl.ds(r0, LANES), :], g)
            for j in range(n_slab):
                u2d[pl.ds(c * (LANES * n_slab) + j, LANES, stride=n_slab), :] = u[:, j * LANES:(j + 1) * LANES]
            u_hi, u_lo = _split2(u)
            lt = _dot_nt(wr_hi, u_hi) + (_dot_nt(wr_hi, u_lo) + _dot_nt(wr_lo, u_hi))
            gl = [lt[k:k + 1, :] for k in range(N_GROUPS)]
            gmax = jnp.maximum(jnp.maximum(gl[0], gl[1]), jnp.maximum(gl[2], gl[3]))
            gsel = jnp.where(gl[0] >= gmax, 0, jnp.where(gl[1] >= gmax, 1, jnp.where(gl[2] >= gmax, 2, 3)))
            denom = (jnp.exp(gl[0] - gmax) + jnp.exp(gl[1] - gmax)) + (jnp.exp(gl[2] - gmax) + jnp.exp(gl[3] - gmax))
            p_top = 1.0 / denom
            le = []
            for k in range(EXPERTS_PER_GROUP):
                rows = [lt[N_GROUPS + gi * EXPERTS_PER_GROUP + k:N_GROUPS + gi * EXPERTS_PER_GROUP + k + 1, :]
                        for gi in range(N_GROUPS)]
                le.append(jnp.where(gsel == 0, rows[0], jnp.where(gsel == 1, rows[1],
                                                                   jnp.where(gsel == 2, rows[2], rows[3]))))
            v1 = jnp.maximum(jnp.maximum(le[0], le[1]), jnp.maximum(le[2], le[3]))
            i1 = jnp.where(le[0] >= v1, 0, jnp.where(le[1] >= v1, 1, jnp.where(le[2] >= v1, 2, 3)))
            rest = [jnp.where(i1 == k, NEG_INF, le[k]) for k in range(EXPERTS_PER_GROUP)]
            v2 = jnp.maximum(jnp.maximum(rest[0], rest[1]), jnp.maximum(rest[2], rest[3]))
            i2 = jnp.where((rest[0] >= v2) & (i1 != 0), 0,
                           jnp.where((rest[1] >= v2) & (i1 != 1), 1,
                                     jnp.where((rest[2] >= v2) & (i1 != 2), 2, 3)))
            ex = jnp.exp(v2 - v1)
            gate1 = p_top / (1.0 + ex)
            gate2 = p_top * ex / (1.0 + ex)
            e1 = gsel * EXPERTS_PER_GROUP + i1
            e2 = gsel * EXPERTS_PER_GROUP + i2
            oh1 = (sub16 == e1).astype(F32)
            oh2 = (sub16 == e2).astype(F32)
            both = oh1 + oh2
            rank = _dot(both.astype(BF16), tri) + carry
            pos1 = jnp.sum(oh1 * rank, axis=0, keepdims=True).astype(I32)
            pos2 = jnp.sum(oh2 * rank, axis=0, keepdims=True).astype(I32)
            rt_i[c] = jnp.zeros((SUBLANES, LANES), I32)
            gates[c] = jnp.zeros((SUBLANES, LANES), F32)
            rt_i[c, 0:1, :] = e1 * cap + pos1
            rt_i[c, 1:2, :] = e2 * cap + pos2
            gates[c, 0:1, :] = gate1
            gates[c, 1:2, :] = gate2
            return carry + jnp.sum(both, axis=1, keepdims=True)

        total = lax.fori_loop(0, n_chunks, chunk, jnp.zeros((N_EXPERTS, 1), F32))
        cnt_v[...] = jnp.broadcast_to(total, (N_EXPERTS, LANES)).astype(I32)
        pltpu.sync_copy(rt_i, rt_si)
        pltpu.sync_copy(cnt_v, cnt_s)

        def invert(c, _):
            for l in range(LANES):
                row = (c * LANES + l) * n_slab
                a1 = rt_si[c, 0, l]
                a2 = rt_si[c, 1, l]
                lst_src[a1] = row
                lst_src[a2] = row
                lst_dst[a1] = row
                lst_dst[a2] = row + blk * n_slab
            return 0

        lax.fori_loop(0, n_chunks, invert, 0)

        for ex in range(N_EXPERTS):
            n = cnt_s[ex, 0]
            n_pad = ((n + (tm - 1)) // tm) * tm

            def pad(i, _, ex=ex, n=n):
                lst_src[ex * cap + i] = lst_src[ex * cap + n - 1]
                lst_dst[ex * cap + i] = lst_dst[ex * cap + n - 1]
                return 0

            lax.fori_loop(n, n_pad, pad, 0)

    n_e = cnt_s[e, 0]
    n_tiles = (n_e + (tm - 1)) // tm

    def tiles(first_tile, bufs):
        bases = [e * cap + (first_tile + t) * tm for t in range(len(bufs))]
        for base, (xb, _) in zip(bases, bufs):
            for r in range(tm):
                src = pl.multiple_of(lst_src[base + r], n_slab)
                xb[pl.ds(r, n_slab, stride=pitch), :] = u2d[pl.ds(src, n_slab), :]
        for base, (xb, ob) in zip(bases, bufs):
            x = jnp.concatenate([xb[j * pitch:j * pitch + tm, :] for j in range(n_slab)], axis=1).astype(BF16)
            hg = _dot(x, wg_ref[0, 0])
            hu = _dot(x, wu_ref[0, 0])
            act = (hg * jax.nn.sigmoid(hg) * hu).astype(BF16)
            o = _dot(act, wd_ref[0, 0])
            for j in range(n_slab):
                ob[j * pitch:j * pitch + tm, :] = o[:, j * LANES:(j + 1) * LANES]
            for r in range(tm):
                dst = pl.multiple_of(lst_dst[base + r], n_slab)
                slots[pl.ds(dst, n_slab), :] = ob[pl.ds(r, n_slab, stride=pitch), :]

    def pair(pi, _):
        tiles(2 * pi, [(x_t, o_t), (x_t2, o_t2)])
        return 0

    lax.fori_loop(0, n_tiles // 2, pair, 0)

    @pl.when(n_tiles % 2 == 1)
    def _():
        tiles(n_tiles - 1, [(x_t, o_t)])

    @pl.when(e == N_EXPERTS - 1)
    def _combine():
        eye = (lax.broadcasted_iota(I32, (LANES, LANES), 0)
               == lax.broadcasted_iota(I32, (LANES, LANES), 1)).astype(BF16)

        def chunk(c, _):
            r0 = pl.multiple_of(c * LANES, LANES)
            g0, g1, g2 = _split3(gates[c])
            gcol = _dot_nt(eye, g0) + (_dot_nt(eye, g1) + _dot_nt(eye, g2))
            ga, gb = gcol[:, 0:1], gcol[:, 1:2]
            for j in range(n_slab):
                a = slots[pl.ds(c * (LANES * n_slab) + j, LANES, stride=n_slab), :]
                b = slots[pl.ds((blk + c * LANES) * n_slab + j, LANES, stride=n_slab), :]
                o_ref[pl.ds(r0, LANES), j * LANES:(j + 1) * LANES] = (
                    h_ref[pl.ds(r0, LANES), j * LANES:(j + 1) * LANES] + (ga * a + gb * b))
            return 0

        lax.fori_loop(0, n_chunks, chunk, 0)


def _moe(h, g, wr_t, wg, wu, wd, *, layer, blk, tm):
    t_rows, d = h.shape
    f = wg.shape[3]
    n_blocks = t_rows // blk
    n_chunks = blk // LANES
    n_slab = d // LANES
    pitch = tm + SUBLANES
    cap = -(-blk // tm) * tm
    kern = functools.partial(_moe_kernel, blk=blk, tm=tm, pitch=pitch, cap=cap)
    return pl.pallas_call(
        kern,
        grid=(n_blocks, N_EXPERTS),
        in_specs=[
            pl.BlockSpec((blk, d), lambda i, e: (i, 0)),
            pl.BlockSpec((1, d), lambda i, e: (0, 0)),
            pl.BlockSpec(wr_t.shape, lambda i, e: (0, 0)),
            pl.BlockSpec((1, 1, d, f), lambda i, e: (layer, e, 0, 0)),
            pl.BlockSpec((1, 1, d, f), lambda i, e: (layer, e, 0, 0)),
            pl.BlockSpec((1, 1, f, d), lambda i, e: (layer, e, 0, 0)),
        ],
        out_specs=pl.BlockSpec((blk, d), lambda i, e: (i, 0), pipeline_mode=pl.Buffered(1)),
        out_shape=jax.ShapeDtypeStruct(h.shape, F32),
        scratch_shapes=[
            pltpu.VMEM((blk * n_slab, LANES), F32),
            pltpu.VMEM((2 * blk * n_slab, LANES), F32),
            pltpu.VMEM((n_slab * pitch, LANES), F32),
            pltpu.VMEM((n_slab * pitch, LANES), F32),
            pltpu.VMEM((n_slab * pitch, LANES), F32),
            pltpu.VMEM((n_slab * pitch, LANES), F32),
            pltpu.VMEM((n_chunks, SUBLANES, LANES), I32),
            pltpu.VMEM((n_chunks, SUBLANES, LANES), F32),
            pltpu.VMEM((N_EXPERTS, LANES), I32),
            pltpu.SMEM((n_chunks, SUBLANES, LANES), I32),
            pltpu.SMEM((N_EXPERTS, LANES), I32),
            pltpu.SMEM((N_EXPERTS * cap,), I32),
            pltpu.SMEM((N_EXPERTS * cap,), I32),
        ],
        input_output_aliases={0: 0},
        compiler_params=_params(("arbitrary", "arbitrary"), VMEM_LIMIT_BYTES),
        name="moe",
    )(h, g, wr_t, wg, wu, wd)


def _kv_rows(x, g_ref, wkv_ref, wf_ref, bf_ref, kg_ref, gg_ref):
    d = x.shape[1]
    u = _rms(x, g_ref[...])
    p = _dot(u.astype(BF16), wkv_ref[...])
    kn = _head_norm(p[:, :d], gg_ref[...], kg_ref[...])
    z = _dot3(u, wf_ref[...])[:, :N_HEADS] + bf_ref[...]
    return kn, p[:, d:], jnp.minimum(z, 0.0) - jnp.log1p(jnp.exp(-jnp.abs(z)))


def _kv_prompt_kernel(h_ref, hm_ref, g_ref, wkv_ref, wf_ref, bf_ref, kg_ref, gg_ref,
                      kp_ref, vp_ref, lp_ref, kb_ref, vb_ref, kbt_ref, vbt_ref, lfm_ref, lft_ref, *, tt):
    i = pl.program_id(1)
    w = (g_ref, wkv_ref, wf_ref, bf_ref, kg_ref, gg_ref)

    @pl.when(i == 0)
    def _():
        kn, v, lf = _kv_rows(hm_ref[...], *w)
        kp_ref[0, 0:N_META, :] = kn
        vp_ref[0, 0:N_META, :] = v
        lp_ref[0, 0:N_META, :] = lf
        kbt_ref[...] = kn.astype(BF16)
        vbt_ref[...] = v.astype(BF16)
        lft_ref[...] = lf

    kn, v, lf = _kv_rows(h_ref[...], *w)
    r0 = pl.multiple_of(N_META + i * tt, SUBLANES)
    kp_ref[0, pl.ds(r0, tt), :] = kn
    vp_ref[0, pl.ds(r0, tt), :] = v
    lp_ref[0, pl.ds(r0, tt), :] = lf
    kb_ref[...] = kn.astype(BF16)
    vb_ref[...] = v.astype(BF16)
    lfm_ref[...] = lf


def _kv_prompt(h, h_meta, g, wkv, wf, bfr, kg, gg, *, n_batch, seq, tt):
    d = h.shape[1]
    n_t = seq // tt
    length = seq + N_META
    const = lambda b, i: (0, 0)
    tile = lambda b, i: (b * n_t + i, 0)
    per_b = lambda b, i: (b, 0)
    whole = lambda b, i: (b, 0, 0)
    return pl.pallas_call(
        functools.partial(_kv_prompt_kernel, tt=tt),
        grid=(n_batch, n_t),
        in_specs=[pl.BlockSpec((tt, d), tile), pl.BlockSpec((N_META, d), per_b),
                  pl.BlockSpec((1, d), const), pl.BlockSpec(wkv.shape, const), pl.BlockSpec(wf.shape, const),
                  pl.BlockSpec((1, N_HEADS), const), pl.BlockSpec((1, d), const), pl.BlockSpec(gg.shape, const)],
        out_specs=[pl.BlockSpec((1, length, d), whole), pl.BlockSpec((1, length, d), whole),
                   pl.BlockSpec((1, length, N_HEADS), whole),
                   pl.BlockSpec((tt, d), tile), pl.BlockSpec((tt, d), tile),
                   pl.BlockSpec((N_META, d), per_b), pl.BlockSpec((N_META, d), per_b),
                   pl.BlockSpec((tt, N_HEADS), tile), pl.BlockSpec((N_META, N_HEADS), per_b)],
        out_shape=[jax.ShapeDtypeStruct((n_batch, length, d), F32), jax.ShapeDtypeStruct((n_batch, length, d), F32),
                   jax.ShapeDtypeStruct((n_batch, length, N_HEADS), F32),
                   jax.ShapeDtypeStruct((n_batch * seq, d), BF16), jax.ShapeDtypeStruct((n_batch * seq, d), BF16),
                   jax.ShapeDtypeStruct((n_batch * N_META, d), BF16), jax.ShapeDtypeStruct((n_batch * N_META, d), BF16),
                   jax.ShapeDtypeStruct((n_batch * seq, N_HEADS), F32),
                   jax.ShapeDtypeStruct((n_batch * N_META, N_HEADS), F32)],
        compiler_params=_params(("arbitrary", "arbitrary"), VMEM_LIMIT_BYTES),
        name="kv_prompt",
    )(h, h_meta, g, wkv, wf, bfr, kg, gg)


def _kv_sample_kernel(h_ref, g_ref, wkv_ref, wf_ref, bf_ref, kg_ref, gg_ref, k_ref, v_ref, lf_ref):
    k_ref[...], v_ref[...], lf_ref[...] = _kv_rows(h_ref[...], g_ref, wkv_ref, wf_ref, bf_ref, kg_ref, gg_ref)


def _kv_sample(h_s, g, wkv, wf, bfr, kg, gg):
    rows, d = h_s.shape
    return pl.pallas_call(
        _kv_sample_kernel,
        out_shape=[jax.ShapeDtypeStruct((rows, d), F32), jax.ShapeDtypeStruct((rows, d), F32),
                   jax.ShapeDtypeStruct((rows, N_HEADS), F32)],
        compiler_params=_params(None, VMEM_LIMIT_BYTES),
        name="kv_sample",
    )(h_s, g, wkv, wf, bfr, kg, gg)


def _q_kernel(h_ref, g_ref, wq_ref, qg_ref, gg_ref, q_ref):
    u = _rms(h_ref[...], g_ref[...])
    q = _dot(u.astype(BF16), wq_ref[...])
    q_ref[...] = (_head_norm(q, gg_ref[...], qg_ref[...]) * ATTN_SCALE).astype(BF16)


def _q_proj(h, g, wq, qg, gg, *, tt):
    t_rows, d = h.shape
    row = lambda i: (i, 0)
    const = lambda i: (0, 0)
    return pl.pallas_call(
        _q_kernel,
        grid=(t_rows // tt,),
        in_specs=[pl.BlockSpec((tt, d), row), pl.BlockSpec((1, d), const), pl.BlockSpec(wq.shape, const),
                  pl.BlockSpec((1, d), const), pl.BlockSpec(gg.shape, const)],
        out_specs=pl.BlockSpec((tt, d), row),
        out_shape=jax.ShapeDtypeStruct((t_rows, d), BF16),
        compiler_params=_params(("arbitrary",), VMEM_LIMIT_BYTES),
        name="q_proj",
    )(h, g, wq, qg, gg)


def _o_kernel(h_ref, am_ref, at_ref, wo_ref, o_ref, *, n_main):
    a = jnp.where(pl.program_id(0) < n_main, am_ref[...], at_ref[...])
    o_ref[...] = h_ref[...] + _dot(a, wo_ref[...])


def _o_proj(h, a_main, a_tail, wo, *, tt):
    t_rows, d = h.shape
    n_main = a_main.shape[0] // tt
    row = lambda i: (i, 0)
    return pl.pallas_call(
        functools.partial(_o_kernel, n_main=n_main),
        grid=(t_rows // tt,),
        in_specs=[pl.BlockSpec((tt, d), row),
                  pl.BlockSpec((tt, d), lambda i: (jnp.minimum(i, n_main - 1), 0)),
                  pl.BlockSpec((tt, d), lambda i: (jnp.maximum(i - n_main, 0), 0)),
                  pl.BlockSpec(wo.shape, lambda i: (0, 0))],
        out_specs=pl.BlockSpec((tt, d), row),
        out_shape=jax.ShapeDtypeStruct(h.shape, F32),
        input_output_aliases={0: 0},
        compiler_params=_params(("arbitrary",), VMEM_LIMIT_BYTES),
        name="o_proj",
    )(h, a_main, a_tail, wo)


def _bias_placement(d):
    h = jnp.arange(N_HEADS)
    base = (h // 2) * LANES + 6 * (h % 2)
    col = jnp.arange(d)[None, None, :]
    i = jnp.arange(3)[:, None, None]
    hit_q = col == (base[None, :, None] + i)
    hit_k = col == (base[None, :, None] + i + 3)
    place_q = hit_q.astype(BF16)
    place_k = -hit_k.astype(BF16)
    ones_q = jnp.any(hit_k, axis=(0, 1)).astype(F32)[None]
    ones_k = jnp.any(hit_q, axis=(0, 1)).astype(F32)[None]
    return place_q, place_k, ones_q, ones_k


def _cumsum_kernel(lm_ref, lt_ref, pq_ref, pk_ref, oq_ref, ok_ref, bqm_ref, bkm_ref, bqt_ref, bkt_ref, ct_ref,
                   *, seq, tc):
    def emit(c, bq_out, bk_out):
        pieces = _split3(c)
        bq = _dot(pieces[0], pq_ref[0]) + (_dot(pieces[1], pq_ref[1]) + _dot(pieces[2], pq_ref[2])) + oq_ref[...]
        bk = _dot(pieces[0], pk_ref[0]) + (_dot(pieces[1], pk_ref[1]) + _dot(pieces[2], pk_ref[2])) + ok_ref[...]
        bq_out(bq.astype(BF16))
        bk_out(bk.astype(BF16))

    cur = jnp.zeros((1, N_HEADS), F32)
    for r in range(N_META):
        cur = cur + lt_ref[r:r + 1, :]
        ct_ref[r:r + 1, :] = cur

    def set_meta_q(x):
        bqt_ref[...] = x

    def set_meta_k(x):
        bkt_ref[...] = x

    emit(ct_ref[...], set_meta_q, set_meta_k)
    tril = (lax.broadcasted_iota(I32, (tc, tc), 1) <= lax.broadcasted_iota(I32, (tc, tc), 0)).astype(BF16)
    for c in range(seq // tc):
        rows = slice(c * tc, (c + 1) * tc)
        blk = _dot_exact_rhs(tril, lm_ref[rows, :]) + cur

        def set_q(x, rows=rows):
            bqm_ref[rows, :] = x

        def set_k(x, rows=rows):
            bkm_ref[rows, :] = x

        emit(blk, set_q, set_k)
        cur = blk[tc - 1:tc, :]


def _cumsum(lf_main, lf_meta, d, *, n_batch, seq, tc):
    kern = functools.partial(_cumsum_kernel, seq=seq, tc=tc)
    place_q, place_k, ones_q, ones_k = _bias_placement(d)
    const3 = lambda b: (0, 0, 0)
    const2 = lambda b: (0, 0)
    return pl.pallas_call(
        kern,
        grid=(n_batch,),
        in_specs=[pl.BlockSpec((seq, N_HEADS), lambda b: (b, 0)), pl.BlockSpec((N_META, N_HEADS), lambda b: (b, 0)),
                  pl.BlockSpec(place_q.shape, const3), pl.BlockSpec(place_k.shape, const3),
                  pl.BlockSpec(ones_q.shape, const2), pl.BlockSpec(ones_k.shape, const2)],
        out_specs=[pl.BlockSpec((seq, d), lambda b: (b, 0)), pl.BlockSpec((seq, d), lambda b: (b, 0)),
                   pl.BlockSpec((N_META, d), lambda b: (b, 0)), pl.BlockSpec((N_META, d), lambda b: (b, 0))],
        out_shape=[jax.ShapeDtypeStruct((n_batch * seq, d), BF16), jax.ShapeDtypeStruct((n_batch * seq, d), BF16),
                   jax.ShapeDtypeStruct((n_batch * N_META, d), BF16), jax.ShapeDtypeStruct((n_batch * N_META, d), BF16)],
        scratch_shapes=[pltpu.VMEM((N_META, N_HEADS), F32)],
        compiler_params=_params(("arbitrary",), VMEM_LIMIT_BYTES),
        name="logf_cumsum",
    )(lf_main, lf_meta, place_q, place_k, ones_q, ones_k)


def _attend(s, vx, mask, m, acc):
    if mask is not None:
        s = jnp.where(mask, s, NEG_INF)
    m_new = jnp.maximum(m, jnp.max(s, axis=1, keepdims=True))
    alpha = jnp.exp(m - m_new)
    p = jnp.exp(s - m_new).astype(BF16)
    return m_new, alpha * acc + _dot(p, vx)


def _attn_kernel(qm_ref, qt_ref, km_ref, kt_ref, vm_ref, vt_ref, bqm_ref, bqt_ref, bkm_ref, bkt_ref,
                 om_ref, ot_ref, *, seq, tq):
    lane = lax.broadcasted_iota(I32, (1, LANES), 1)
    first = lane < HEAD_DIM
    one = jnp.ones((), BF16)

    def q_heads(q2, bq):
        keep = [lane < 6, (lane >= 6) & (lane < 12)]
        return [jnp.concatenate([jnp.where(first, q2, 0) if hd == 0 else jnp.where(first, 0, q2),
                                 jnp.where(keep[hd], bq, 0)], axis=1) for hd in range(2)]

    def v_heads(v2):
        return [jnp.where(first, v2, one), jnp.where(first, one, v2)]

    def finish(accs):
        outs = [acc / pltpu.roll(acc, HEAD_DIM, axis=1) for acc in accs]
        return jnp.where(first, outs[0], outs[1]).astype(BF16)

    def init(rows):
        return jnp.full((rows, 1), NEG_INF, F32), jnp.zeros((rows, LANES), F32)

    def scores(qa, ka):
        return _dot_nt(qa[0], ka), _dot_nt(qa[1], ka)

    def update(s, vx, mask, state):
        return _attend(s[0], vx[0], mask, state[0], state[1]) + _attend(s[1], vx[1], mask, state[2], state[3])

    def k_rows(c0):
        return jnp.concatenate([km_ref[pl.ds(c0, tq), :], bkm_ref[pl.ds(c0, tq), :]], axis=1)

    ka_t = jnp.concatenate([kt_ref[...], bkt_ref[...]], axis=1)
    vx_t = v_heads(vt_ref[...])

    causal_t = (lax.broadcasted_iota(I32, (N_META, N_META), 1) <= lax.broadcasted_iota(I32, (N_META, N_META), 0))
    st = update(scores(q_heads(qt_ref[...], bqt_ref[...]), ka_t), vx_t, causal_t, init(N_META) + init(N_META))
    ot_ref[...] = finish([st[1], st[3]])

    causal = (lax.broadcasted_iota(I32, (tq, tq), 1) <= lax.broadcasted_iota(I32, (tq, tq), 0))
    for qi in range(seq // tq):
        r0 = qi * tq
        qa = q_heads(qm_ref[r0:r0 + tq, :], bqm_ref[r0:r0 + tq, :])
        state = update(scores(qa, ka_t), vx_t, None, init(tq) + init(tq))

        s = scores(qa, k_rows(0))
        for j in range(qi):
            s_next = scores(qa, k_rows((j + 1) * tq))
            state = update(s, v_heads(vm_ref[j * tq:(j + 1) * tq, :]), None, state)
            s = s_next
        state = update(s, v_heads(vm_ref[r0:r0 + tq, :]), causal, state)
        om_ref[r0:r0 + tq, :] = finish([state[1], state[3]])


def _prompt_attn(q, k, k_meta, v, v_meta, bq_main, bq_meta, bk_main, bk_meta, *, n_batch, seq, tq):
    t_rows, d = q.shape
    n_pairs = d // LANES
    meta0 = (n_batch * seq) // N_META
    main = pl.BlockSpec((seq, LANES), lambda b, p: (b, p))
    meta = pl.BlockSpec((N_META, LANES), lambda b, p: (meta0 + b, p))
    small = pl.BlockSpec((N_META, LANES), lambda b, p: (b, p))
    kern = functools.partial(_attn_kernel, seq=seq, tq=tq)
    return pl.pallas_call(
        kern,
        grid=(n_batch, n_pairs),
        in_specs=[main, meta, main, small, main, small, main, small, main, small],
        out_specs=[main, small],
        out_shape=[jax.ShapeDtypeStruct((n_batch * seq, d), BF16), jax.ShapeDtypeStruct((n_batch * N_META, d), BF16)],
        compiler_params=_params(("arbitrary", "arbitrary"), VMEM_LIMIT_BYTES),
        name="prompt_attn",
    )(q, q, k, k_meta, v, v_meta, bq_main, bq_meta, bk_main, bk_meta)


def _decode_kernel(pt_ref, q_ref, *refs, n_steps, n_new, ppg):
    kc_refs, vc_refs, lc_refs = refs[:ppg], refs[ppg:2 * ppg], refs[2 * ppg:3 * ppg]
    kn_ref, vn_ref, ln_ref, o_ref, qbd, qbd_b, csq, m_s, l_s, acc_s, run_s = refs[3 * ppg:]
    j = pl.program_id(1)
    d = q_ref.shape[2]
    rows = n_new * N_HEADS
    head_of_col = lax.broadcasted_iota(I32, (N_HEADS, d), 1) // HEAD_DIM
    head_of_row = lax.broadcasted_iota(I32, (N_HEADS, d), 0)
    diag = head_of_col == head_of_row

    @pl.when(j == 0)
    def _():
        for qi in range(n_new):
            qbd[qi * N_HEADS:(qi + 1) * N_HEADS, :] = jnp.where(diag, q_ref[0, qi:qi + 1, :].astype(F32), 0.0)
        qbd_b[...] = qbd[...].astype(BF16)
        cur = jnp.zeros((N_HEADS, 1), F32)
        for qi in range(n_new):
            cur = cur + ln_ref[0, :, qi:qi + 1]
            csq[qi * N_HEADS:(qi + 1) * N_HEADS, :] = cur
        m_s[...] = jnp.full(m_s.shape, NEG_INF, F32)
        l_s[...] = jnp.zeros(l_s.shape, F32)
        acc_s[...] = jnp.zeros(acc_s.shape, F32)
        run_s[...] = jnp.zeros(run_s.shape, F32)

    page = lc_refs[0].shape[2]
    later = (lax.broadcasted_iota(I32, (page, page), 0) > lax.broadcasted_iota(I32, (page, page), 1)).astype(BF16)
    qb = qbd_b[...]
    run = run_s[...]
    scores = []
    for pg in range(ppg):
        lf = lc_refs[pg][0]
        l0, l1, l2 = _split3(lf)
        suffix = _dot(l0, later) + (_dot(l1, later) + _dot(l2, later)) + run
        run = run + jnp.sum(lf, axis=1, keepdims=True)
        scores.append(_dot(qb, kc_refs[pg][0].astype(BF16)) + jnp.concatenate([suffix] * n_new, axis=0))
    run_s[...] = run
    s = jnp.concatenate(scores, axis=1) + csq[...]
    m_new = jnp.maximum(m_s[...], jnp.max(s, axis=1, keepdims=True))
    alpha = jnp.exp(m_s[...] - m_new)
    p32 = jnp.exp(s - m_new)
    l_s[...] = alpha * l_s[...] + jnp.sum(p32, axis=1, keepdims=True)
    p = p32.astype(BF16)
    pv =_dot_nt(p[:, 0:page], vc_refs[0][0].astype(BF16))
    for pg in range(1, ppg):
        pv = pv + _dot_nt(p[:, pg * page:(pg + 1) * page], vc_refs[pg][0].astype(BF16))
    acc_s[...] = alpha * acc_s[...] + pv
    m_s[...] = m_new

    @pl.when(j == n_steps - 1)
    def _():
        row_q = lax.broadcasted_iota(I32, (rows, 1), 0) // N_HEADS
        m, l, acc = m_s[...], l_s[...], acc_s[...]
        qf = qbd[...]
        for jn in range(n_new):
            kn = kn_ref[0, jn:jn + 1, :]
            s = jnp.sum(qf * kn, axis=1, keepdims=True)
            cj = csq[jn * N_HEADS:(jn + 1) * N_HEADS, :]
            s = s + (csq[...] - jnp.concatenate([cj] * n_new, axis=0))
            s = jnp.where(row_q >= jn, s, NEG_INF)
            m_new = jnp.maximum(m, s)
            alpha = jnp.exp(m - m_new)
            p = jnp.exp(s - m_new)
            l = alpha * l + p
            acc = alpha * acc + p * vn_ref[0, jn:jn + 1, :]
            m = m_new
        out = acc / l
        for qi in range(n_new):
            blk = jnp.where(diag, out[qi * N_HEADS:(qi + 1) * N_HEADS, :], 0.0)
            o_ref[0, qi:qi + 1, :] = jnp.sum(blk, axis=0, keepdims=True).astype(o_ref.dtype)


def _decode_attn(page_table, q, kc, vc, lc, kn, vn, ln_t):
    n_seq, n_new, d = q.shape
    n_pages = page_table.shape[1]
    page = kc.shape[2]
    rows = n_new * N_HEADS
    ppg = _pick((n_pages,), (8, 4, 2, 1))
    n_steps = n_pages // ppg
    kern = functools.partial(_decode_kernel, n_steps=n_steps, n_new=n_new, ppg=ppg)
    seq3 = lambda n, j, pt: (n, 0, 0)

    def pg3(pg):
        return lambda n, j, pt: (pt[n, n_pages - 1 - (j * ppg + pg)], 0, 0)

    grid_spec = pltpu.PrefetchScalarGridSpec(
        num_scalar_prefetch=1,
        grid=(n_seq, n_steps),
        in_specs=([pl.BlockSpec((1, n_new, d), seq3)]
                  + [pl.BlockSpec((1, d, page), pg3(pg)) for pg in range(ppg)]
                  + [pl.BlockSpec((1, d, page), pg3(pg)) for pg in range(ppg)]
                  + [pl.BlockSpec((1, N_HEADS, page), pg3(pg)) for pg in range(ppg)]
                  + [pl.BlockSpec((1, n_new, d), seq3), pl.BlockSpec((1, n_new, d), seq3),
                     pl.BlockSpec((1, N_HEADS, n_new), seq3)]),
        out_specs=pl.BlockSpec((1, n_new, d), seq3),
        scratch_shapes=[pltpu.VMEM((rows, d), F32), pltpu.VMEM((rows, d), BF16), pltpu.VMEM((rows, 1), F32),
                        pltpu.VMEM((rows, 1), F32), pltpu.VMEM((rows, 1), F32), pltpu.VMEM((rows, d), F32),
                        pltpu.VMEM((N_HEADS, 1), F32)],
    )
    return pl.pallas_call(
        kern,
        grid_spec=grid_spec,
        out_shape=jax.ShapeDtypeStruct((n_seq, n_new, d), F32),
        compiler_params=_params(("arbitrary", "arbitrary"), VMEM_LIMIT_BYTES),
        name="decode_attn",
    )(page_table, q, *([kc] * ppg), *([vc] * ppg), *([lc] * ppg), kn, vn, ln_t)


def _pick(totals, cands):
    for c in cands:
        if all(t % c == 0 for t in totals):
            return c
    raise ValueError(f"no tile among {cands} divides {totals}")


def kernel(x_prompt, x_sample, state_pool, cache_k, cache_v, cache_logf, page_table, meta, pool_norm_g, pool_w,
           pool_scale, kv_norm_g, w_kvf, b_f, k_norm_g, attn_norm_g, w_q, q_norm_g, w_o, ffn_norm_g,
           w_router_group, w_router_expert, w_gate, w_up, w_down):
    n_batch, seq, d = x_prompt.shape
    n_seq, n_new, _ = x_sample.shape
    depth = ffn_norm_g.shape[0]
    n_a = pool_norm_g.shape[0]
    n_phys, page = cache_k.shape[0], cache_k.shape[1]
    past_len = page_table.shape[1] * page
    r_main, r_meta, r_s = n_batch * seq, n_batch * N_META, n_seq * n_new
    t_rows = r_main + r_meta + r_s
    assert d == N_HEADS * HEAD_DIM and d % (LANES * SUBLANES) == 0 and n_batch % SUBLANES == 0 and n_seq % SUBLANES == 0

    tt_pool = _pick((seq, t_rows), (512, 256, 128))
    tt_proj = _pick((t_rows,), (640, 512, 256, 128))
    tt_o = _pick((r_main, r_meta + r_s), (256, 128))
    blk_moe = _pick((t_rows,), (1664, 1280, 1024, 512, 256, 128))
    tq = _pick((seq,), (512, 256, 128))
    tc = _pick((seq,), (256, 128))

    xs_t = jnp.transpose(x_sample, (1, 0, 2))
    h = jnp.concatenate([x_prompt.reshape(r_main, d),
                         jnp.broadcast_to(meta[None], (n_batch, N_META, d)).reshape(r_meta, d),
                         xs_t.reshape(r_s, d)], axis=0)

    gg = (jnp.arange(MXU_DIM)[:, None] // HEAD_DIM == jnp.arange(MXU_DIM)[None, :] // HEAD_DIM).astype(BF16)
    state_t = jnp.transpose(state_pool, (0, 2, 1, 3))
    zero_prev = jnp.zeros((POOL_STATE, n_batch, d), F32)

    def put(hbuf, rows, start):
        return lax.dynamic_update_slice(hbuf, rows, (start, 0))

    def moe(hbuf, layer):
        wr = jnp.concatenate([w_router_group[layer].T,
                              jnp.transpose(w_router_expert[layer], (0, 2, 1)).reshape(N_EXPERTS, d),
                              jnp.zeros((32 - N_GROUPS - N_EXPERTS, d), F32)], axis=0)
        return _moe(hbuf, ffn_norm_g[layer][None], wr, wg_b, wu_b, wd_b, layer=layer, blk=blk_moe, tm=128)

    wg_b, wu_b, wd_b = w_gate.astype(BF16), w_up.astype(BF16), w_down.astype(BF16)

    tails_p, tails_s = [], []
    for layer in range(n_a):
        g = pool_norm_g[layer][None]
        sc = pool_scale[layer][None]
        w = pool_w[layer]
        h_meta = h[r_main:r_main + r_meta]
        h_s = h[r_main + r_meta:]
        h, tail = _pool_main(h, h_meta, g, w, sc, n_batch=n_batch, seq=seq, tt=tt_pool)
        meta_t = jnp.transpose(h_meta.reshape(n_batch, N_META, d), (1, 0, 2))
        y_meta, _ = _pool_small(meta_t, zero_prev, g, w, sc, pos0=0)
        y_s, u_s = _pool_small(h_s.reshape(n_new, n_seq, d), state_t[layer], g, w, sc, pos0=past_len)
        h = put(h, jnp.transpose(y_meta, (1, 0, 2)).reshape(r_meta, d), r_main)
        h = put(h, y_s.reshape(r_s, d), r_main + r_meta)
        tails_p.append(jnp.transpose(tail[:, 1:], (1, 0, 2)))
        tails_s.append(jnp.concatenate([state_t[layer][n_new:], u_s], axis=0))
        h = moe(h, layer)

    wkv = w_kvf[:, :2 * d].astype(BF16)
    wf = jnp.pad(w_kvf[:, 2 * d:], ((0, 0), (0, LANES - N_HEADS)))
    s0 = r_main + r_meta
    kv_w = (kv_norm_g[None], wkv, wf, b_f[None], jnp.tile(k_norm_g, N_HEADS)[None], gg)
    k_p, v_p, logf_p, kb, vb, kb_meta, vb_meta, lf_main, lf_meta = _kv_prompt(
        h, h[r_main:s0], *kv_w, n_batch=n_batch, seq=seq, tt=tt_pool)
    k_sr, v_sr, lf_sr = _kv_sample(h[s0:], *kv_w)
    bq_main, bk_main, bq_meta, bk_meta = _cumsum(lf_main, lf_meta, d, n_batch=n_batch, seq=seq, tc=tc)

    kc = jnp.transpose(cache_k, (0, 2, 3, 1)).reshape(n_phys, d, page)
    vc = jnp.transpose(cache_v, (0, 2, 3, 1)).reshape(n_phys, d, page)
    lc = jnp.transpose(cache_logf, (0, 2, 1))

    def to_seq_major(rows):
        return jnp.transpose(rows.reshape(n_new, n_seq, rows.shape[-1]), (1, 0, 2))

    k_s, v_s, lf_s = to_seq_major(k_sr), to_seq_major(v_sr), to_seq_major(lf_sr)
    lf_s_t = jnp.transpose(lf_s, (0, 2, 1))

    for layer in range(n_a, depth):
        jb = layer - n_a
        q = _q_proj(h, attn_norm_g[jb][None], w_q[jb].astype(BF16), jnp.tile(q_norm_g[jb], N_HEADS)[None], gg, tt=tt_proj)
        a_main, a_meta = _prompt_attn(q, kb, kb_meta, vb, vb_meta, bq_main, bq_meta, bk_main, bk_meta,
                                      n_batch=n_batch, seq=seq, tq=tq)
        a_s = _decode_attn(page_table, to_seq_major(q[s0:]).astype(F32), kc, vc, lc, k_s, v_s, lf_s_t)
        a_tail = jnp.concatenate([a_meta, jnp.transpose(a_s, (1, 0, 2)).reshape(r_s, d).astype(BF16)], axis=0)
        h = _o_proj(h, a_main, a_tail, w_o[jb].astype(BF16), tt=tt_o)
        h = moe(h, layer)

    y_prompt = h[:r_main].reshape(n_batch, seq, d)
    y_sample = to_seq_major(h[s0:])
    pool_state_prompt = jnp.transpose(jnp.stack(tails_p, axis=0), (0, 2, 1, 3))
    pool_state_sample = jnp.transpose(jnp.stack(tails_s, axis=0), (0, 2, 1, 3))
    k_p = k_p.reshape(n_batch, seq + N_META, N_HEADS, HEAD_DIM)
    v_p = v_p.reshape(n_batch, seq + N_META, N_HEADS, HEAD_DIM)
    return (y_prompt, y_sample, pool_state_prompt, pool_state_sample, k_p, v_p, logf_p,
            k_s.reshape(n_seq, n_new, N_HEADS, HEAD_DIM), v_s.reshape(n_seq, n_new, N_HEADS, HEAD_DIM), lf_s)
```

```python
import functools

import jax
import jax.numpy as jnp
from jax import lax
from jax.experimental import pallas as pl
from jax.experimental.pallas import tpu as pltpu

F32 = jnp.float32
BF16 = jnp.bfloat16
I32 = jnp.int32

N_META = 16
POOL_WINDOWS = (2, 4, 8, 16)
POOL_STATE = max(POOL_WINDOWS) - 1
N_HEADS = 16
HEAD_DIM = 64
N_GROUPS = 4
EXPERTS_PER_GROUP = 4
N_EXPERTS = N_GROUPS * EXPERTS_PER_GROUP
ATTN_SCALE = HEAD_DIM ** -0.5
EPS = 1e-6
NEG_INF = -1e30

LANES = 128
SUBLANES = 8
MXU_DIM = 256
VMEM_LIMIT_BYTES = 56 * 1024 * 1024


def _params(sem, vmem=None):
    return pltpu.CompilerParams(dimension_semantics=sem, vmem_limit_bytes=vmem)


def _rms(x, g):
    return x * lax.rsqrt(jnp.mean(x * x, axis=-1, keepdims=True) + EPS) * g


def _dot(a, b):
    return jnp.dot(a, b, preferred_element_type=F32)


def _dot_nt(a, b):
    return lax.dot_general(a, b, (((1,), (1,)), ((), ())), preferred_element_type=F32)


def _split2(a):
    hi = a.astype(BF16)
    lo = (a - hi.astype(F32)).astype(BF16)
    return hi, lo


def _split3(a):
    hi = a.astype(BF16)
    r = a - hi.astype(F32)
    mid = r.astype(BF16)
    lo = (r - mid.astype(F32)).astype(BF16)
    return hi, mid, lo


def _dot3(a, b):
    ah, al = _split2(a)
    bh, bl = _split2(b)
    return _dot(ah, bh) + (_dot(ah, bl) + _dot(al, bh))


def _dot_exact_rhs(mask_bf16, b):
    b0, b1, b2 = _split3(b)
    return _dot(mask_bf16, b0) + (_dot(mask_bf16, b1) + _dot(mask_bf16, b2))


def _head_norm(k, gg, gain):
    ksq = k * k
    hi, lo = _split2(ksq)
    parts = []
    for c in range(k.shape[1] // MXU_DIM):
        sl = slice(c * MXU_DIM, (c + 1) * MXU_DIM)
        parts.append(_dot(hi[:, sl], gg) + _dot(lo[:, sl], gg))
    ms = jnp.concatenate(parts, axis=1) * (1.0 / HEAD_DIM)
    return k * lax.rsqrt(ms + EPS) * gain


def _pool_main_kernel(h_ref, hm_ref, g_ref, whi_ref, wlo_ref, sc_ref, o_ref, tail_ref, ext_ref, sa_ref, sb_ref,
                      *, tt, n_t):
    i = pl.program_id(1)
    g = g_ref[...]
    d = h_ref.shape[1]
    gd = d // len(POOL_WINDOWS)
    hist = 2 * N_META
    end = hist + tt

    @pl.when(i == 0)
    def _():
        ext_ref[0:N_META, :] = jnp.zeros((N_META, d), F32)
        ext_ref[N_META:hist, :] = _rms(hm_ref[...], g)

    x = h_ref[...]
    u = _rms(x, g)
    ext_ref[hist:end, :] = u
    sa_ref[8:end, :] = ext_ref[8:end, :] + ext_ref[7:end - 1, :]
    sb_ref[16:end, gd:] = sa_ref[16:end, gd:] + sa_ref[14:end - 2, gd:]
    sa_ref[24:end, 2 * gd:] = sb_ref[24:end, 2 * gd:] + sb_ref[20:end - 4, 2 * gd:]
    sums = [sa_ref[hist:end, 0:gd], sb_ref[hist:end, gd:2 * gd], sa_ref[hist:end, 2 * gd:3 * gd],
            sa_ref[hist:end, 3 * gd:] + sa_ref[hist - 8:end - 8, 3 * gd:]]
    outs = []
    for gi, w in enumerate(POOL_WINDOWS):
        dh, dl = _split2(sums[gi] * (1.0 / w) - u[:, gi * gd:(gi + 1) * gd])
        outs.append(_dot(dh, whi_ref[gi]) + (_dot(dh, wlo_ref[gi]) + _dot(dl, whi_ref[gi])))
    o_ref[...] = x + jnp.concatenate(outs, axis=1) * sc_ref[...]
    ext_ref[0:hist, :] = ext_ref[tt:end, :]

    @pl.when(i == n_t - 1)
    def _():
        tail_ref[0] = u[tt - N_META:tt, :]


def _pool_main(h, h_meta, g, w_hi, w_lo, sc, *, n_batch, seq, tt):
    t_rows, d = h.shape
    n_t = seq // tt
    kern = functools.partial(_pool_main_kernel, tt=tt, n_t=n_t)
    return pl.pallas_call(
        kern,
        grid=(n_batch, n_t),
        in_specs=[
            pl.BlockSpec((tt, d), lambda b, i: (b * n_t + i, 0)),
            pl.BlockSpec((N_META, d), lambda b, i: (b, 0)),
            pl.BlockSpec((1, d), lambda b, i: (0, 0)),
            pl.BlockSpec(w_hi.shape, lambda b, i: (0, 0, 0)),
            pl.BlockSpec(w_lo.shape, lambda b, i: (0, 0, 0)),
            pl.BlockSpec((1, d), lambda b, i: (0, 0)),
        ],
        out_specs=[
            pl.BlockSpec((tt, d), lambda b, i: (b * n_t + i, 0)),
            pl.BlockSpec((1, N_META, d), lambda b, i: (b, 0, 0)),
        ],
        out_shape=[jax.ShapeDtypeStruct(h.shape, F32), jax.ShapeDtypeStruct((n_batch, N_META, d), F32)],
        scratch_shapes=[pltpu.VMEM((2 * N_META + tt, d), F32)] * 3,
        input_output_aliases={0: 0},
        compiler_params=_params(("arbitrary", "arbitrary"), VMEM_LIMIT_BYTES),
        name="pool_main",
    )(h, h_meta, g, w_hi, w_lo, sc)


def _pool_small_kernel(x_ref, prev_ref, g_ref, w_ref, sc_ref, y_ref, u_ref, *, n_new, pos0):
    g = g_ref[...]
    n_seq, d = x_ref.shape[1], x_ref.shape[2]
    gd = d // len(POOL_WINDOWS)
    us = [_rms(x_ref[t], g) for t in range(n_new)]
    ext = [prev_ref[j] for j in range(POOL_STATE)] + us
    outs = []
    for gi, w in enumerate(POOL_WINDOWS):
        lo, hi = gi * gd, (gi + 1) * gd
        diffs = []
        for t in range(n_new):
            ug = us[t][:, lo:hi]
            acc = ug
            for k in range(1, w):
                acc = acc + ext[POOL_STATE + t - k][:, lo:hi]
            cnt = float(min(pos0 + t + 1, w))
            diffs.append(acc / cnt - ug)
        outs.append(_dot3(jnp.concatenate(diffs, axis=0), w_ref[gi]))
    out = jnp.concatenate(outs, axis=1) * sc_ref[...]
    for t in range(n_new):
        y_ref[t] = x_ref[t] + out[t * n_seq:(t + 1) * n_seq, :]
        u_ref[t] = us[t]


def _pool_small(x, prev, g, w, sc, *, pos0):
    n_new = x.shape[0]
    kern = functools.partial(_pool_small_kernel, n_new=n_new, pos0=pos0)
    return pl.pallas_call(
        kern,
        out_shape=[jax.ShapeDtypeStruct(x.shape, F32), jax.ShapeDtypeStruct(x.shape, F32)],
        compiler_params=_params(None, VMEM_LIMIT_BYTES),
        name="pool_small",
    )(x, prev, g, w, sc)


def _moe_kernel(h_ref, g_ref, wr_ref, wg_ref, wu_ref, wd_ref, o_ref,
                u2d, slots, x_t, o_t, x_t2, o_t2, rt_i, gates, cnt_v, rt_si, cnt_s, lst_src, lst_dst,
                *, blk, tm, pitch, cap):
    e = pl.program_id(1)
    n_chunks = blk // LANES
    d = h_ref.shape[1]
    n_slab = d // LANES

    @pl.when(e == 0)
    def _router():
        g = g_ref[...]
        wr_hi, wr_lo = _split2(wr_ref[...])
        sub16 = lax.broadcasted_iota(I32, (N_EXPERTS, LANES), 0)
        tri = (lax.broadcasted_iota(I32, (LANES, LANES), 0)
               < lax.broadcasted_iota(I32, (LANES, LANES), 1)).astype(BF16)

        def chunk(c, carry):
            r0 = pl.multiple_of(c * LANES, LANES)
            u = _rms(h_ref[pl.ds(r0, LANES), :], g)
            for j in range(n_slab):
                u2d[pl.ds(c * (LANES * n_slab) + j, LANES, stride=n_slab), :] = u[:, j * LANES:(j + 1) * LANES]
            u_hi, u_lo = _split2(u)
            lt = _dot_nt(wr_hi, u_hi) + (_dot_nt(wr_hi, u_lo) + _dot_nt(wr_lo, u_hi))
            gl = [lt[k:k + 1, :] for k in range(N_GROUPS)]
            gmax = jnp.maximum(jnp.maximum(gl[0], gl[1]), jnp.maximum(gl[2], gl[3]))
            gsel = jnp.where(gl[0] >= gmax, 0, jnp.where(gl[1] >= gmax, 1, jnp.where(gl[2] >= gmax, 2, 3)))
            denom = (jnp.exp(gl[0] - gmax) + jnp.exp(gl[1] - gmax)) + (jnp.exp(gl[2] - gmax) + jnp.exp(gl[3] - gmax))
            p_top = 1.0 / denom
            le = []
            for k in range(EXPERTS_PER_GROUP):
                rows = [lt[N_GROUPS + gi * EXPERTS_PER_GROUP + k:N_GROUPS + gi * EXPERTS_PER_GROUP + k + 1, :]
                        for gi in range(N_GROUPS)]
                le.append(jnp.where(gsel == 0, rows[0], jnp.where(gsel == 1, rows[1],
                                                                   jnp.where(gsel == 2, rows[2], rows[3]))))
            v1 = jnp.maximum(jnp.maximum(le[0], le[1]), jnp.maximum(le[2], le[3]))
            i1 = jnp.where(le[0] >= v1, 0, jnp.where(le[1] >= v1, 1, jnp.where(le[2] >= v1, 2, 3)))
            rest = [jnp.where(i1 == k, NEG_INF, le[k]) for k in range(EXPERTS_PER_GROUP)]
            v2 = jnp.maximum(jnp.maximum(rest[0], rest[1]), jnp.maximum(rest[2], rest[3]))
            i2 = jnp.where((rest[0] >= v2) & (i1 != 0), 0,
                           jnp.where((rest[1] >= v2) & (i1 != 1), 1,
                                     jnp.where((rest[2] >= v2) & (i1 != 2), 2, 3)))
            ex = jnp.exp(v2 - v1)
            gate1 = p_top / (1.0 + ex)
            gate2 = p_top * ex / (1.0 + ex)
            e1 = gsel * EXPERTS_PER_GROUP + i1
            e2 = gsel * EXPERTS_PER_GROUP + i2
            oh1 = (sub16 == e1).astype(F32)
            oh2 = (sub16 == e2).astype(F32)
            both = oh1 + oh2
            rank = _dot(both.astype(BF16), tri) + carry
            pos1 = jnp.sum(oh1 * rank, axis=0, keepdims=True).astype(I32)
            pos2 = jnp.sum(oh2 * rank, axis=0, keepdims=True).astype(I32)
            rt_i[c] = jnp.zeros((SUBLANES, LANES), I32)
            gates[c] = jnp.zeros((SUBLANES, LANES), F32)
            rt_i[c, 0:1, :] = e1 * cap + pos1
            rt_i[c, 1:2, :] = e2 * cap + pos2
            gates[c, 0:1, :] = gate1
            gates[c, 1:2, :] = gate2
            return carry + jnp.sum(both, axis=1, keepdims=True)

        total = lax.fori_loop(0, n_chunks, chunk, jnp.zeros((N_EXPERTS, 1), F32))
        cnt_v[...] = jnp.broadcast_to(total, (N_EXPERTS, LANES)).astype(I32)
        pltpu.sync_copy(rt_i, rt_si)
        pltpu.sync_copy(cnt_v, cnt_s)

        def invert(c, _):
            for l in range(LANES):
                row = (c * LANES + l) * n_slab
                a1 = rt_si[c, 0, l]
                a2 = rt_si[c, 1, l]
                lst_src[a1] = row
                lst_src[a2] = row
                lst_dst[a1] = row
                lst_dst[a2] = row + blk * n_slab
            return 0

        lax.fori_loop(0, n_chunks, invert, 0)

        for ex in range(N_EXPERTS):
            n = cnt_s[ex, 0]
            n_pad = ((n + (tm - 1)) // tm) * tm

            def pad(i, _, ex=ex, n=n):
                lst_src[ex * cap + i] = lst_src[ex * cap + n - 1]
                lst_dst[ex * cap + i] = lst_dst[ex * cap + n - 1]
                return 0

            lax.fori_loop(n, n_pad, pad, 0)

    n_e = cnt_s[e, 0]
    n_tiles = (n_e + (tm - 1)) // tm

    def tiles(first_tile, bufs):
        bases = [e * cap + (first_tile + t) * tm for t in range(len(bufs))]
        for base, (xb, _) in zip(bases, bufs):
            for r in range(tm):
                src = pl.multiple_of(lst_src[base + r], n_slab)
                xb[pl.ds(r, n_slab, stride=pitch), :] = u2d[pl.ds(src, n_slab), :]
        for base, (xb, ob) in zip(bases, bufs):
            x = jnp.concatenate([xb[j * pitch:j * pitch + tm, :] for j in range(n_slab)], axis=1).astype(BF16)
            hg = _dot(x, wg_ref[0, 0])
            hu = _dot(x, wu_ref[0, 0])
            act = (hg * jax.nn.sigmoid(hg) * hu).astype(BF16)
            o = _dot(act, wd_ref[0, 0])
            for j in range(n_slab):
                ob[j * pitch:j * pitch + tm, :] = o[:, j * LANES:(j + 1) * LANES]
            for r in range(tm):
                dst = pl.multiple_of(lst_dst[base + r], n_slab)
                slots[pl.ds(dst, n_slab), :] = ob[pl.ds(r, n_slab, stride=pitch), :]

    def pair(pi, _):
        tiles(2 * pi, [(x_t, o_t), (x_t2, o_t2)])
        return 0

    lax.fori_loop(0, n_tiles // 2, pair, 0)

    @pl.when(n_tiles % 2 == 1)
    def _():
        tiles(n_tiles - 1, [(x_t, o_t)])

    @pl.when(e == N_EXPERTS - 1)
    def _combine():
        eye = (lax.broadcasted_iota(I32, (LANES, LANES), 0)
               == lax.broadcasted_iota(I32, (LANES, LANES), 1)).astype(BF16)

        def chunk(c, _):
            r0 = pl.multiple_of(c * LANES, LANES)
            g0, g1, g2 = _split3(gates[c])
            gcol = _dot_nt(eye, g0) + (_dot_nt(eye, g1) + _dot_nt(eye, g2))
            ga, gb = gcol[:, 0:1], gcol[:, 1:2]
            for j in range(n_slab):
                a = slots[pl.ds(c * (LANES * n_slab) + j, LANES, stride=n_slab), :]
                b = slots[pl.ds((blk + c * LANES) * n_slab + j, LANES, stride=n_slab), :]
                o_ref[pl.ds(r0, LANES), j * LANES:(j + 1) * LANES] = (
                    h_ref[pl.ds(r0, LANES), j * LANES:(j + 1) * LANES] + (ga * a + gb * b))
            return 0

        lax.fori_loop(0, n_chunks, chunk, 0)


def _moe(h, g, wr_t, wg, wu, wd, *, layer, blk, tm):
    t_rows, d = h.shape
    f = wg.shape[3]
    n_blocks = t_rows // blk
    n_chunks = blk // LANES
    n_slab = d // LANES
    pitch = tm + SUBLANES
    cap = -(-blk // tm) * tm
    kern = functools.partial(_moe_kernel, blk=blk, tm=tm, pitch=pitch, cap=cap)
    return pl.pallas_call(
        kern,
        grid=(n_blocks, N_EXPERTS),
        in_specs=[
            pl.BlockSpec((blk, d), lambda i, e: (i, 0)),
            pl.BlockSpec((1, d), lambda i, e: (0, 0)),
            pl.BlockSpec(wr_t.shape, lambda i, e: (0, 0)),
            pl.BlockSpec((1, 1, d, f), lambda i, e: (layer, e, 0, 0)),
            pl.BlockSpec((1, 1, d, f), lambda i, e: (layer, e, 0, 0)),
            pl.BlockSpec((1, 1, f, d), lambda i, e: (layer, e, 0, 0)),
        ],
        out_specs=pl.BlockSpec((blk, d), lambda i, e: (i, 0), pipeline_mode=pl.Buffered(1)),
        out_shape=jax.ShapeDtypeStruct(h.shape, F32),
        scratch_shapes=[
            pltpu.VMEM((blk * n_slab, LANES), F32),
            pltpu.VMEM((2 * blk * n_slab, LANES), F32),
            pltpu.VMEM((n_slab * pitch, LANES), F32),
            pltpu.VMEM((n_slab * pitch, LANES), F32),
            pltpu.VMEM((n_slab * pitch, LANES), F32),
            pltpu.VMEM((n_slab * pitch, LANES), F32),
            pltpu.VMEM((n_chunks, SUBLANES, LANES), I32),
            pltpu.VMEM((n_chunks, SUBLANES, LANES), F32),
            pltpu.VMEM((N_EXPERTS, LANES), I32),
            pltpu.SMEM((n_chunks, SUBLANES, LANES), I32),
            pltpu.SMEM((N_EXPERTS, LANES), I32),
            pltpu.SMEM((N_EXPERTS * cap,), I32),
            pltpu.SMEM((N_EXPERTS * cap,), I32),
        ],
        input_output_aliases={0: 0},
        compiler_params=_params(("arbitrary", "arbitrary"), VMEM_LIMIT_BYTES),
        name="moe",
    )(h, g, wr_t, wg, wu, wd)


def _kv_rows(x, g_ref, wkv_ref, wf_ref, bf_ref, kg_ref, gg_ref):
    d = x.shape[1]
    u = _rms(x, g_ref[...])
    p = _dot(u.astype(BF16), wkv_ref[...])
    kn = _head_norm(p[:, :d], gg_ref[...], kg_ref[...])
    z = _dot3(u, wf_ref[...])[:, :N_HEADS] + bf_ref[...]
    return kn, p[:, d:], jnp.minimum(z, 0.0) - jnp.log1p(jnp.exp(-jnp.abs(z)))


def _kv_prompt_kernel(h_ref, hm_ref, g_ref, wkv_ref, wf_ref, bf_ref, kg_ref, gg_ref,
                      kp_ref, vp_ref, lp_ref, kb_ref, vb_ref, kbt_ref, vbt_ref, lfm_ref, lft_ref, *, tt):
    i = pl.program_id(1)
    w = (g_ref, wkv_ref, wf_ref, bf_ref, kg_ref, gg_ref)

    @pl.when(i == 0)
    def _():
        kn, v, lf = _kv_rows(hm_ref[...], *w)
        kp_ref[0, 0:N_META, :] = kn
        vp_ref[0, 0:N_META, :] = v
        lp_ref[0, 0:N_META, :] = lf
        kbt_ref[...] = kn.astype(BF16)
        vbt_ref[...] = v.astype(BF16)
        lft_ref[...] = lf

    kn, v, lf = _kv_rows(h_ref[...], *w)
    r0 = pl.multiple_of(N_META + i * tt, SUBLANES)
    kp_ref[0, pl.ds(r0, tt), :] = kn
    vp_ref[0, pl.ds(r0, tt), :] = v
    lp_ref[0, pl.ds(r0, tt), :] = lf
    kb_ref[...] = kn.astype(BF16)
    vb_ref[...] = v.astype(BF16)
    lfm_ref[...] = lf


def _kv_prompt(h, h_meta, g, wkv, wf, bfr, kg, gg, *, n_batch, seq, tt):
    d = h.shape[1]
    n_t = seq // tt
    length = seq + N_META
    const = lambda b, i: (0, 0)
    tile = lambda b, i: (b * n_t + i, 0)
    per_b = lambda b, i: (b, 0)
    whole = lambda b, i: (b, 0, 0)
    return pl.pallas_call(
        functools.partial(_kv_prompt_kernel, tt=tt),
        grid=(n_batch, n_t),
        in_specs=[pl.BlockSpec((tt, d), tile), pl.BlockSpec((N_META, d), per_b),
                  pl.BlockSpec((1, d), const), pl.BlockSpec(wkv.shape, const), pl.BlockSpec(wf.shape, const),
                  pl.BlockSpec((1, N_HEADS), const), pl.BlockSpec((1, d), const), pl.BlockSpec(gg.shape, const)],
        out_specs=[pl.BlockSpec((1, length, d), whole), pl.BlockSpec((1, length, d), whole),
                   pl.BlockSpec((1, length, N_HEADS), whole),
                   pl.BlockSpec((tt, d), tile), pl.BlockSpec((tt, d), tile),
                   pl.BlockSpec((N_META, d), per_b), pl.BlockSpec((N_META, d), per_b),
                   pl.BlockSpec((tt, N_HEADS), tile), pl.BlockSpec((N_META, N_HEADS), per_b)],
        out_shape=[jax.ShapeDtypeStruct((n_batch, length, d), F32), jax.ShapeDtypeStruct((n_batch, length, d), F32),
                   jax.ShapeDtypeStruct((n_batch, length, N_HEADS), F32),
                   jax.ShapeDtypeStruct((n_batch * seq, d), BF16), jax.ShapeDtypeStruct((n_batch * seq, d), BF16),
                   jax.ShapeDtypeStruct((n_batch * N_META, d), BF16), jax.ShapeDtypeStruct((n_batch * N_META, d), BF16),
                   jax.ShapeDtypeStruct((n_batch * seq, N_HEADS), F32),
                   jax.ShapeDtypeStruct((n_batch * N_META, N_HEADS), F32)],
        compiler_params=_params(("arbitrary", "arbitrary"), VMEM_LIMIT_BYTES),
        name="kv_prompt",
    )(h, h_meta, g, wkv, wf, bfr, kg, gg)


def _kv_sample_kernel(h_ref, g_ref, wkv_ref, wf_ref, bf_ref, kg_ref, gg_ref, k_ref, v_ref, lf_ref):
    k_ref[...], v_ref[...], lf_ref[...] = _kv_rows(h_ref[...], g_ref, wkv_ref, wf_ref, bf_ref, kg_ref, gg_ref)


def _kv_sample(h_s, g, wkv, wf, bfr, kg, gg):
    rows, d = h_s.shape
    return pl.pallas_call(
        _kv_sample_kernel,
        out_shape=[jax.ShapeDtypeStruct((rows, d), F32), jax.ShapeDtypeStruct((rows, d), F32),
                   jax.ShapeDtypeStruct((rows, N_HEADS), F32)],
        compiler_params=_params(None, VMEM_LIMIT_BYTES),
        name="kv_sample",
    )(h_s, g, wkv, wf, bfr, kg, gg)


def _q_kernel(h_ref, g_ref, wq_ref, qg_ref, gg_ref, q_ref):
    u = _rms(h_ref[...], g_ref[...])
    q = _dot(u.astype(BF16), wq_ref[...])
    q_ref[...] = (_head_norm(q, gg_ref[...], qg_ref[...]) * ATTN_SCALE).astype(BF16)


def _q_proj(h, g, wq, qg, gg, *, tt):
    t_rows, d = h.shape
    row = lambda i: (i, 0)
    const = lambda i: (0, 0)
    return pl.pallas_call(
        _q_kernel,
        grid=(t_rows // tt,),
        in_specs=[pl.BlockSpec((tt, d), row), pl.BlockSpec((1, d), const), pl.BlockSpec(wq.shape, const),
                  pl.BlockSpec((1, d), const), pl.BlockSpec(gg.shape, const)],
        out_specs=pl.BlockSpec((tt, d), row),
        out_shape=jax.ShapeDtypeStruct((t_rows, d), BF16),
        compiler_params=_params(("arbitrary",), VMEM_LIMIT_BYTES),
        name="q_proj",
    )(h, g, wq, qg, gg)


def _o_kernel(h_ref, am_ref, at_ref, wo_ref, o_ref, *, n_main):
    a = jnp.where(pl.program_id(0) < n_main, am_ref[...], at_ref[...])
    o_ref[...] = h_ref[...] + _dot(a, wo_ref[...])


def _o_proj(h, a_main, a_tail, wo, *, tt):
    t_rows, d = h.shape
    n_main = a_main.shape[0] // tt
    row = lambda i: (i, 0)
    return pl.pallas_call(
        functools.partial(_o_kernel, n_main=n_main),
        grid=(t_rows // tt,),
        in_specs=[pl.BlockSpec((tt, d), row),
                  pl.BlockSpec((tt, d), lambda i: (jnp.minimum(i, n_main - 1), 0)),
                  pl.BlockSpec((tt, d), lambda i: (jnp.maximum(i - n_main, 0), 0)),
                  pl.BlockSpec(wo.shape, lambda i: (0, 0))],
        out_specs=pl.BlockSpec((tt, d), row),
        out_shape=jax.ShapeDtypeStruct(h.shape, F32),
        input_output_aliases={0: 0},
        compiler_params=_params(("arbitrary",), VMEM_LIMIT_BYTES),
        name="o_proj",
    )(h, a_main, a_tail, wo)


def _bias_placement(d):
    h = jnp.arange(N_HEADS)
    base = (h // 2) * LANES + 6 * (h % 2)
    col = jnp.arange(d)[None, None, :]
    i = jnp.arange(3)[:, None, None]
    hit_q = col == (base[None, :, None] + i)
    hit_k = col == (base[None, :, None] + i + 3)
    place_q = hit_q.astype(BF16)
    place_k = -hit_k.astype(BF16)
    ones_q = jnp.any(hit_k, axis=(0, 1)).astype(F32)[None]
    ones_k = jnp.any(hit_q, axis=(0, 1)).astype(F32)[None]
    return place_q, place_k, ones_q, ones_k


def _cumsum_kernel(lm_ref, lt_ref, pq_ref, pk_ref, oq_ref, ok_ref, bqm_ref, bkm_ref, bqt_ref, bkt_ref, ct_ref,
                   *, seq, tc):
    def emit(c, bq_out, bk_out):
        pieces = _split3(c)
        bq = _dot(pieces[0], pq_ref[0]) + (_dot(pieces[1], pq_ref[1]) + _dot(pieces[2], pq_ref[2])) + oq_ref[...]
        bk = _dot(pieces[0], pk_ref[0]) + (_dot(pieces[1], pk_ref[1]) + _dot(pieces[2], pk_ref[2])) + ok_ref[...]
        bq_out(bq.astype(BF16))
        bk_out(bk.astype(BF16))

    cur = jnp.zeros((1, N_HEADS), F32)
    for r in range(N_META):
        cur = cur + lt_ref[r:r + 1, :]
        ct_ref[r:r + 1, :] = cur

    def set_meta_q(x):
        bqt_ref[...] = x

    def set_meta_k(x):
        bkt_ref[...] = x

    emit(ct_ref[...], set_meta_q, set_meta_k)
    tril = (lax.broadcasted_iota(I32, (tc, tc), 1) <= lax.broadcasted_iota(I32, (tc, tc), 0)).astype(BF16)
    for c in range(seq // tc):
        rows = slice(c * tc, (c + 1) * tc)
        blk = _dot_exact_rhs(tril, lm_ref[rows, :]) + cur

        def set_q(x, rows=rows):
            bqm_ref[rows, :] = x

        def set_k(x, rows=rows):
            bkm_ref[rows, :] = x

        emit(blk, set_q, set_k)
        cur = blk[tc - 1:tc, :]


def _cumsum(lf_main, lf_meta, d, *, n_batch, seq, tc):
    kern = functools.partial(_cumsum_kernel, seq=seq, tc=tc)
    place_q, place_k, ones_q, ones_k = _bias_placement(d)
    const3 = lambda b: (0, 0, 0)
    const2 = lambda b: (0, 0)
    return pl.pallas_call(
        kern,
        grid=(n_batch,),
        in_specs=[pl.BlockSpec((seq, N_HEADS), lambda b: (b, 0)), pl.BlockSpec((N_META, N_HEADS), lambda b: (b, 0)),
                  pl.BlockSpec(place_q.shape, const3), pl.BlockSpec(place_k.shape, const3),
                  pl.BlockSpec(ones_q.shape, const2), pl.BlockSpec(ones_k.shape, const2)],
        out_specs=[pl.BlockSpec((seq, d), lambda b: (b, 0)), pl.BlockSpec((seq, d), lambda b: (b, 0)),
                   pl.BlockSpec((N_META, d), lambda b: (b, 0)), pl.BlockSpec((N_META, d), lambda b: (b, 0))],
        out_shape=[jax.ShapeDtypeStruct((n_batch * seq, d), BF16), jax.ShapeDtypeStruct((n_batch * seq, d), BF16),
                   jax.ShapeDtypeStruct((n_batch * N_META, d), BF16), jax.ShapeDtypeStruct((n_batch * N_META, d), BF16)],
        scratch_shapes=[pltpu.VMEM((N_META, N_HEADS), F32)],
        compiler_params=_params(("arbitrary",), VMEM_LIMIT_BYTES),
        name="logf_cumsum",
    )(lf_main, lf_meta, place_q, place_k, ones_q, ones_k)


def _attend(s, vx, mask, m, acc):
    if mask is not None:
        s = jnp.where(mask, s, NEG_INF)
    m_new = jnp.maximum(m, jnp.max(s, axis=1, keepdims=True))
    alpha = jnp.exp(m - m_new)
    p = jnp.exp(s - m_new).astype(BF16)
    return m_new, alpha * acc + _dot(p, vx)


def _attn_kernel(qm_ref, qt_ref, km_ref, kt_ref, vm_ref, vt_ref, bqm_ref, bqt_ref, bkm_ref, bkt_ref,
                 om_ref, ot_ref, *, seq, tq):
    lane = lax.broadcasted_iota(I32, (1, LANES), 1)
    first = lane < HEAD_DIM
    one = jnp.ones((), BF16)

    def q_heads(q2, bq):
        keep = [lane < 6, (lane >= 6) & (lane < 12)]
        return [jnp.concatenate([jnp.where(first, q2, 0) if hd == 0 else jnp.where(first, 0, q2),
                                 jnp.where(keep[hd], bq, 0)], axis=1) for hd in range(2)]

    def v_heads(v2):
        return [jnp.where(first, v2, one), jnp.where(first, one, v2)]

    def finish(accs):
        outs = [acc / pltpu.roll(acc, HEAD_DIM, axis=1) for acc in accs]
        return jnp.where(first, outs[0], outs[1]).astype(BF16)

    def init(rows):
        return jnp.full((rows, 1), NEG_INF, F32), jnp.zeros((rows, LANES), F32)

    def scores(qa, ka):
        return _dot_nt(qa[0], ka), _dot_nt(qa[1], ka)

    def update(s, vx, mask, state):
        return _attend(s[0], vx[0], mask, state[0], state[1]) + _attend(s[1], vx[1], mask, state[2], state[3])

    def k_rows(c0):
        return jnp.concatenate([km_ref[pl.ds(c0, tq), :], bkm_ref[pl.ds(c0, tq), :]], axis=1)

    ka_t = jnp.concatenate([kt_ref[...], bkt_ref[...]], axis=1)
    vx_t = v_heads(vt_ref[...])

    causal_t = (lax.broadcasted_iota(I32, (N_META, N_META), 1) <= lax.broadcasted_iota(I32, (N_META, N_META), 0))
    st = update(scores(q_heads(qt_ref[...], bqt_ref[...]), ka_t), vx_t, causal_t, init(N_META) + init(N_META))
    ot_ref[...] = finish([st[1], st[3]])

    causal = (lax.broadcasted_iota(I32, (tq, tq), 1) <= lax.broadcasted_iota(I32, (tq, tq), 0))
    for qi in range(seq // tq):
        r0 = qi * tq
        qa = q_heads(qm_ref[r0:r0 + tq, :], bqm_ref[r0:r0 + tq, :])
        state = update(scores(qa, ka_t), vx_t, None, init(tq) + init(tq))

        s = scores(qa, k_rows(0))
        for j in range(qi):
            s_next = scores(qa, k_rows((j + 1) * tq))
            state = update(s, v_heads(vm_ref[j * tq:(j + 1) * tq, :]), None, state)
            s = s_next
        state = update(s, v_heads(vm_ref[r0:r0 + tq, :]), causal, state)
        om_ref[r0:r0 + tq, :] = finish([state[1], state[3]])


def _prompt_attn(q, k, k_meta, v, v_meta, bq_main, bq_meta, bk_main, bk_meta, *, n_batch, seq, tq):
    t_rows, d = q.shape
    n_pairs = d // LANES
    meta0 = (n_batch * seq) // N_META
    main = pl.BlockSpec((seq, LANES), lambda b, p: (b, p))
    meta = pl.BlockSpec((N_META, LANES), lambda b, p: (meta0 + b, p))
    small = pl.BlockSpec((N_META, LANES), lambda b, p: (b, p))
    kern = functools.partial(_attn_kernel, seq=seq, tq=tq)
    return pl.pallas_call(
        kern,
        grid=(n_batch, n_pairs),
        in_specs=[main, meta, main, small, main, small, main, small, main, small],
        out_specs=[main, small],
        out_shape=[jax.ShapeDtypeStruct((n_batch * seq, d), BF16), jax.ShapeDtypeStruct((n_batch * N_META, d), BF16)],
        compiler_params=_params(("arbitrary", "arbitrary"), VMEM_LIMIT_BYTES),
        name="prompt_attn",
    )(q, q, k, k_meta, v, v_meta, bq_main, bq_meta, bk_main, bk_meta)


def _decode_kernel(pt_ref, q_ref, *refs, n_steps, n_new, ppg):
    kc_refs, vc_refs, lc_refs = refs[:ppg], refs[ppg:2 * ppg], refs[2 * ppg:3 * ppg]
    kn_ref, vn_ref, ln_ref, o_ref, qbd, qbd_b, csq, m_s, l_s, acc_s, run_s = refs[3 * ppg:]
    j = pl.program_id(1)
    d = q_ref.shape[2]
    rows = n_new * N_HEADS
    head_of_col = lax.broadcasted_iota(I32, (N_HEADS, d), 1) // HEAD_DIM
    head_of_row = lax.broadcasted_iota(I32, (N_HEADS, d), 0)
    diag = head_of_col == head_of_row

    @pl.when(j == 0)
    def _():
        for qi in range(n_new):
            qbd[qi * N_HEADS:(qi + 1) * N_HEADS, :] = jnp.where(diag, q_ref[0, qi:qi + 1, :].astype(F32), 0.0)
        qbd_b[...] = qbd[...].astype(BF16)
        cur = jnp.zeros((N_HEADS, 1), F32)
        for qi in range(n_new):
            cur = cur + ln_ref[0, :, qi:qi + 1]
            csq[qi * N_HEADS:(qi + 1) * N_HEADS, :] = cur
        m_s[...] = jnp.full(m_s.shape, NEG_INF, F32)
        l_s[...] = jnp.zeros(l_s.shape, F32)
        acc_s[...] = jnp.zeros(acc_s.shape, F32)
        run_s[...] = jnp.zeros(run_s.shape, F32)

    page = lc_refs[0].shape[2]
    later = (lax.broadcasted_iota(I32, (page, page), 0) > lax.broadcasted_iota(I32, (page, page), 1)).astype(BF16)
    qb = qbd_b[...]
    run = run_s[...]
    scores = []
    for pg in range(ppg):
        lf = lc_refs[pg][0]
        l0, l1, l2 = _split3(lf)
        suffix = _dot(l0, later) + (_dot(l1, later) + _dot(l2, later)) + run
        run = run + jnp.sum(lf, axis=1, keepdims=True)
        scores.append(_dot(qb, kc_refs[pg][0].astype(BF16)) + jnp.concatenate([suffix] * n_new, axis=0))
    run_s[...] = run
    s = jnp.concatenate(scores, axis=1) + csq[...]
    m_new = jnp.maximum(m_s[...], jnp.max(s, axis=1, keepdims=True))
    alpha = jnp.exp(m_s[...] - m_new)
    p32 = jnp.exp(s - m_new)
    l_s[...] = alpha * l_s[...] + jnp.sum(p32, axis=1, keepdims=True)
    p = p32.astype(BF16)
    pv =_dot_nt(p[:, 0:page], vc_refs[0][0].astype(BF16))
    for pg in range(1, ppg):
        pv = pv + _dot_nt(p[:, pg * page:(pg + 1) * page], vc_refs[pg][0].astype(BF16))
    acc_s[...] = alpha * acc_s[...] + pv
    m_s[...] = m_new

    @pl.when(j == n_steps - 1)
    def _():
        row_q = lax.broadcasted_iota(I32, (rows, 1), 0) // N_HEADS
        m, l, acc = m_s[...], l_s[...], acc_s[...]
        qf = qbd[...]
        for jn in range(n_new):
            kn = kn_ref[0, jn:jn + 1, :]
            s = jnp.sum(qf * kn, axis=1, keepdims=True)
            cj = csq[jn * N_HEADS:(jn + 1) * N_HEADS, :]
            s = s + (csq[...] - jnp.concatenate([cj] * n_new, axis=0))
            s = jnp.where(row_q >= jn, s, NEG_INF)
            m_new = jnp.maximum(m, s)
            alpha = jnp.exp(m - m_new)
            p = jnp.exp(s - m_new)
            l = alpha * l + p
            acc = alpha * acc + p * vn_ref[0, jn:jn + 1, :]
            m = m_new
        out = acc / l
        for qi in range(n_new):
            blk = jnp.where(diag, out[qi * N_HEADS:(qi + 1) * N_HEADS, :], 0.0)
            o_ref[0, qi:qi + 1, :] = jnp.sum(blk, axis=0, keepdims=True).astype(o_ref.dtype)


def _decode_attn(page_table, q, kc, vc, lc, kn, vn, ln_t):
    n_seq, n_new, d = q.shape
    n_pages = page_table.shape[1]
    page = kc.shape[2]
    rows = n_new * N_HEADS
    ppg = _pick((n_pages,), (16, 8, 4, 2, 1))
    n_steps = n_pages // ppg
    kern = functools.partial(_decode_kernel, n_steps=n_steps, n_new=n_new, ppg=ppg)
    seq3 = lambda n, j, pt: (n, 0, 0)

    def pg3(pg):
        return lambda n, j, pt: (pt[n, n_pages - 1 - (j * ppg + pg)], 0, 0)

    grid_spec = pltpu.PrefetchScalarGridSpec(
        num_scalar_prefetch=1,
        grid=(n_seq, n_steps),
        in_specs=([pl.BlockSpec((1, n_new, d), seq3)]
                  + [pl.BlockSpec((1, d, page), pg3(pg)) for pg in range(ppg)]
                  + [pl.BlockSpec((1, d, page), pg3(pg)) for pg in range(ppg)]
                  + [pl.BlockSpec((1, N_HEADS, page), pg3(pg)) for pg in range(ppg)]
                  + [pl.BlockSpec((1, n_new, d), seq3), pl.BlockSpec((1, n_new, d), seq3),
                     pl.BlockSpec((1, N_HEADS, n_new), seq3)]),
        out_specs=pl.BlockSpec((1, n_new, d), seq3),
        scratch_shapes=[pltpu.VMEM((rows, d), F32), pltpu.VMEM((rows, d), BF16), pltpu.VMEM((rows, 1), F32),
                        pltpu.VMEM((rows, 1), F32), pltpu.VMEM((rows, 1), F32), pltpu.VMEM((rows, d), F32),
                        pltpu.VMEM((N_HEADS, 1), F32)],
    )
    return pl.pallas_call(
        kern,
        grid_spec=grid_spec,
        out_shape=jax.ShapeDtypeStruct((n_seq, n_new, d), F32),
        compiler_params=_params(("arbitrary", "arbitrary"), VMEM_LIMIT_BYTES),
        name="decode_attn",
    )(page_table, q, *([kc] * ppg), *([vc] * ppg), *([lc] * ppg), kn, vn, ln_t)


def _pick(totals, cands):
    for c in cands:
        if all(t % c == 0 for t in totals):
            return c
    raise ValueError(f"no tile among {cands} divides {totals}")


def kernel(x_prompt, x_sample, state_pool, cache_k, cache_v, cache_logf, page_table, meta, pool_norm_g, pool_w,
           pool_scale, kv_norm_g, w_kvf, b_f, k_norm_g, attn_norm_g, w_q, q_norm_g, w_o, ffn_norm_g,
           w_router_group, w_router_expert, w_gate, w_up, w_down):
    n_batch, seq, d = x_prompt.shape
    n_seq, n_new, _ = x_sample.shape
    depth = ffn_norm_g.shape[0]
    n_a = pool_norm_g.shape[0]
    n_phys, page = cache_k.shape[0], cache_k.shape[1]
    past_len = page_table.shape[1] * page
    r_main, r_meta, r_s = n_batch * seq, n_batch * N_META, n_seq * n_new
    t_rows = r_main + r_meta + r_s
    assert d == N_HEADS * HEAD_DIM and d % (LANES * SUBLANES) == 0 and n_batch % SUBLANES == 0 and n_seq % SUBLANES == 0

    tt_pool = _pick((seq, t_rows), (512, 256, 128))
    tt_proj = _pick((t_rows,), (640, 512, 256, 128))
    tt_o = _pick((r_main, r_meta + r_s), (256, 128))
    blk_moe = _pick((t_rows,), (1664, 1280, 1024, 512, 256, 128))
    tq = _pick((seq,), (512, 256, 128))
    tc = _pick((seq,), (256, 128))

    xs_t = jnp.transpose(x_sample, (1, 0, 2))
    h = jnp.concatenate([x_prompt.reshape(r_main, d),
                         jnp.broadcast_to(meta[None], (n_batch, N_META, d)).reshape(r_meta, d),
                         xs_t.reshape(r_s, d)], axis=0)

    gg = (jnp.arange(MXU_DIM)[:, None] // HEAD_DIM == jnp.arange(MXU_DIM)[None, :] // HEAD_DIM).astype(BF16)
    state_t = jnp.transpose(state_pool, (0, 2, 1, 3))
    zero_prev = jnp.zeros((POOL_STATE, n_batch, d), F32)

    def put(hbuf, rows, start):
        return lax.dynamic_update_slice(hbuf, rows, (start, 0))

    def moe(hbuf, layer):
        wr = jnp.concatenate([w_router_group[layer].T,
                              jnp.transpose(w_router_expert[layer], (0, 2, 1)).reshape(N_EXPERTS, d),
                              jnp.zeros((32 - N_GROUPS - N_EXPERTS, d), F32)], axis=0)
        return _moe(hbuf, ffn_norm_g[layer][None], wr, wg_b, wu_b, wd_b, layer=layer, blk=blk_moe, tm=256)

    wg_b, wu_b, wd_b = w_gate.astype(BF16), w_up.astype(BF16), w_down.astype(BF16)

    tails_p, tails_s = [], []
    for layer in range(n_a):
        g = pool_norm_g[layer][None]
        sc = pool_scale[layer][None]
        w = pool_w[layer]
        h_meta = h[r_main:r_main + r_meta]
        h_s = h[r_main + r_meta:]
        w_hi = w.astype(BF16)
        w_lo = (w - w_hi.astype(F32)).astype(BF16)
        h, tail = _pool_main(h, h_meta, g, w_hi, w_lo, sc, n_batch=n_batch, seq=seq, tt=tt_pool)
        meta_t = jnp.transpose(h_meta.reshape(n_batch, N_META, d), (1, 0, 2))
        y_meta, _ = _pool_small(meta_t, zero_prev, g, w, sc, pos0=0)
        y_s, u_s = _pool_small(h_s.reshape(n_new, n_seq, d), state_t[layer], g, w, sc, pos0=past_len)
        h = put(h, jnp.transpose(y_meta, (1, 0, 2)).reshape(r_meta, d), r_main)
        h = put(h, y_s.reshape(r_s, d), r_main + r_meta)
        tails_p.append(jnp.transpose(tail[:, 1:], (1, 0, 2)))
        tails_s.append(jnp.concatenate([state_t[layer][n_new:], u_s], axis=0))
        h = moe(h, layer)

    wkv = w_kvf[:, :2 * d].astype(BF16)
    wf = jnp.pad(w_kvf[:, 2 * d:], ((0, 0), (0, LANES - N_HEADS)))
    s0 = r_main + r_meta
    kv_w = (kv_norm_g[None], wkv, wf, b_f[None], jnp.tile(k_norm_g, N_HEADS)[None], gg)
    k_p, v_p, logf_p, kb, vb, kb_meta, vb_meta, lf_main, lf_meta = _kv_prompt(
        h, h[r_main:s0], *kv_w, n_batch=n_batch, seq=seq, tt=tt_pool)
    k_sr, v_sr, lf_sr = _kv_sample(h[s0:], *kv_w)
    bq_main, bk_main, bq_meta, bk_meta = _cumsum(lf_main, lf_meta, d, n_batch=n_batch, seq=seq, tc=tc)

    kc = jnp.transpose(cache_k, (0, 2, 3, 1)).reshape(n_phys, d, page)
    vc = jnp.transpose(cache_v, (0, 2, 3, 1)).reshape(n_phys, d, page)
    lc = jnp.transpose(cache_logf, (0, 2, 1))

    def to_seq_major(rows):
        return jnp.transpose(rows.reshape(n_new, n_seq, rows.shape[-1]), (1, 0, 2))

    k_s, v_s, lf_s = to_seq_major(k_sr), to_seq_major(v_sr), to_seq_major(lf_sr)
    lf_s_t = jnp.transpose(lf_s, (0, 2, 1))

    for layer in range(n_a, depth):
        jb = layer - n_a
        q = _q_proj(h, attn_norm_g[jb][None], w_q[jb].astype(BF16), jnp.tile(q_norm_g[jb], N_HEADS)[None], gg, tt=tt_proj)
        a_main, a_meta = _prompt_attn(q, kb, kb_meta, vb, vb_meta, bq_main, bq_meta, bk_main, bk_meta,
                                      n_batch=n_batch, seq=seq, tq=tq)
        a_s = _decode_attn(page_table, to_seq_major(q[s0:]).astype(F32), kc, vc, lc, k_s, v_s, lf_s_t)
        a_tail = jnp.concatenate([a_meta, jnp.transpose(a_s, (1, 0, 2)).reshape(r_s, d).astype(BF16)], axis=0)
        h = _o_proj(h, a_main, a_tail, w_o[jb].astype(BF16), tt=tt_o)
        h = moe(h, layer)

    y_prompt = h[:r_main].reshape(n_batch, seq, d)
    y_sample = to_seq_major(h[s0:])
    pool_state_prompt = jnp.transpose(jnp.stack(tails_p, axis=0), (0, 2, 1, 3))
    pool_state_sample = jnp.transpose(jnp.stack(tails_s, axis=0), (0, 2, 1, 3))
    k_p = k_p.reshape(n_batch, seq + N_META, N_HEADS, HEAD_DIM)
    v_p = v_p.reshape(n_batch, seq + N_META, N_HEADS, HEAD_DIM)
    return (y_prompt, y_sample, pool_state_prompt, pool_state_sample, k_p, v_p, logf_p,
            k_s.reshape(n_seq, n_new, N_HEADS, HEAD_DIM), v_s.reshape(n_seq, n_new, N_HEADS, HEAD_DIM), lf_s)
```

```python
import functools

import jax
import jax.numpy as jnp
from jax import lax
from jax.experimental import pallas as pl
from jax.experimental.pallas import tpu as pltpu

F32 = jnp.float32
BF16 = jnp.bfloat16
I32 = jnp.int32

N_META = 16
POOL_WINDOWS = (2, 4, 8, 16)
POOL_STATE = max(POOL_WINDOWS) - 1
N_HEADS = 16
HEAD_DIM = 64
N_GROUPS = 4
EXPERTS_PER_GROUP = 4
N_EXPERTS = N_GROUPS * EXPERTS_PER_GROUP
ATTN_SCALE = HEAD_DIM ** -0.5
EPS = 1e-6
NEG_INF = -1e30

LANES = 128
SUBLANES = 8
MXU_DIM = 256
VMEM_LIMIT_BYTES = 56 * 1024 * 1024


def _params(sem, vmem=None):
    return pltpu.CompilerParams(dimension_semantics=sem, vmem_limit_bytes=vmem)


def _rms(x, g):
    return x * lax.rsqrt(jnp.mean(x * x, axis=-1, keepdims=True) + EPS) * g


def _dot(a, b):
    return jnp.dot(a, b, preferred_element_type=F32)


def _dot_nt(a, b):
    return lax.dot_general(a, b, (((1,), (1,)), ((), ())), preferred_element_type=F32)


def _split2(a):
    hi = a.astype(BF16)
    lo = (a - hi.astype(F32)).astype(BF16)
    return hi, lo


def _split3(a):
    hi = a.astype(BF16)
    r = a - hi.astype(F32)
    mid = r.astype(BF16)
    lo = (r - mid.astype(F32)).astype(BF16)
    return hi, mid, lo


def _dot3(a, b):
    ah, al = _split2(a)
    bh, bl = _split2(b)
    return _dot(ah, bh) + (_dot(ah, bl) + _dot(al, bh))


def _dot_exact_rhs(mask_bf16, b):
    b0, b1, b2 = _split3(b)
    return _dot(mask_bf16, b0) + (_dot(mask_bf16, b1) + _dot(mask_bf16, b2))


def _head_norm(k, gg, gain):
    ksq = k * k
    hi, lo = _split2(ksq)
    parts = []
    for c in range(k.shape[1] // MXU_DIM):
        sl = slice(c * MXU_DIM, (c + 1) * MXU_DIM)
        parts.append(_dot(hi[:, sl], gg) + _dot(lo[:, sl], gg))
    ms = jnp.concatenate(parts, axis=1) * (1.0 / HEAD_DIM)
    return k * lax.rsqrt(ms + EPS) * gain


def _pool_main_kernel(h_ref, hm_ref, g_ref, w_ref, sc_ref, o_ref, tail_ref, ext_ref, *, tt, n_t):
    i = pl.program_id(1)
    g = g_ref[...]
    gd = h_ref.shape[1] // len(POOL_WINDOWS)

    @pl.when(i == 0)
    def _():
        ext_ref[0:N_META, :] = _rms(hm_ref[...], g)

    x = h_ref[...]
    u = _rms(x, g)
    ext_ref[N_META:N_META + tt, :] = u
    outs = []
    for gi, w in enumerate(POOL_WINDOWS):
        lo, hi = gi * gd, (gi + 1) * gd
        ug = u[:, lo:hi]
        acc = ug
        for k in range(1, w):
            acc = acc + ext_ref[N_META - k:N_META - k + tt, lo:hi]
        diff = acc * (1.0 / w) - ug
        outs.append(_dot3(diff, w_ref[gi]))
    o_ref[...] = x + jnp.concatenate(outs, axis=1) * sc_ref[...]
    ext_ref[0:N_META, :] = ext_ref[tt:tt + N_META, :]

    @pl.when(i == n_t - 1)
    def _():
        tail_ref[0] = u[tt - N_META:tt, :]


def _pool_main(h, h_meta, g, w, sc, *, n_batch, seq, tt):
    t_rows, d = h.shape
    n_t = seq // tt
    kern = functools.partial(_pool_main_kernel, tt=tt, n_t=n_t)
    return pl.pallas_call(
        kern,
        grid=(n_batch, n_t),
        in_specs=[
            pl.BlockSpec((tt, d), lambda b, i: (b * n_t + i, 0)),
            pl.BlockSpec((N_META, d), lambda b, i: (b, 0)),
            pl.BlockSpec((1, d), lambda b, i: (0, 0)),
            pl.BlockSpec(w.shape, lambda b, i: (0, 0, 0)),
            pl.BlockSpec((1, d), lambda b, i: (0, 0)),
        ],
        out_specs=[
            pl.BlockSpec((tt, d), lambda b, i: (b * n_t + i, 0)),
            pl.BlockSpec((1, N_META, d), lambda b, i: (b, 0, 0)),
        ],
        out_shape=[jax.ShapeDtypeStruct(h.shape, F32), jax.ShapeDtypeStruct((n_batch, N_META, d), F32)],
        scratch_shapes=[pltpu.VMEM((N_META + tt, d), F32)],
        input_output_aliases={0: 0},
        compiler_params=_params(("arbitrary", "arbitrary"), VMEM_LIMIT_BYTES),
        name="pool_main",
    )(h, h_meta, g, w, sc)


def _pool_small_kernel(x_ref, prev_ref, g_ref, w_ref, sc_ref, y_ref, u_ref, *, n_new, pos0):
    g = g_ref[...]
    n_seq, d = x_ref.shape[1], x_ref.shape[2]
    gd = d // len(POOL_WINDOWS)
    us = [_rms(x_ref[t], g) for t in range(n_new)]
    ext = [prev_ref[j] for j in range(POOL_STATE)] + us
    outs = []
    for gi, w in enumerate(POOL_WINDOWS):
        lo, hi = gi * gd, (gi + 1) * gd
        diffs = []
        for t in range(n_new):
            ug = us[t][:, lo:hi]
            acc = ug
            for k in range(1, w):
                acc = acc + ext[POOL_STATE + t - k][:, lo:hi]
            cnt = float(min(pos0 + t + 1, w))
            diffs.append(acc / cnt - ug)
        outs.append(_dot3(jnp.concatenate(diffs, axis=0), w_ref[gi]))
    out = jnp.concatenate(outs, axis=1) * sc_ref[...]
    for t in range(n_new):
        y_ref[t] = x_ref[t] + out[t * n_seq:(t + 1) * n_seq, :]
        u_ref[t] = us[t]


def _pool_small(x, prev, g, w, sc, *, pos0):
    n_new = x.shape[0]
    kern = functools.partial(_pool_small_kernel, n_new=n_new, pos0=pos0)
    return pl.pallas_call(
        kern,
        out_shape=[jax.ShapeDtypeStruct(x.shape, F32), jax.ShapeDtypeStruct(x.shape, F32)],
        compiler_params=_params(None, VMEM_LIMIT_BYTES),
        name="pool_small",
    )(x, prev, g, w, sc)


def _moe_kernel(h_ref, g_ref, wr_ref, wg_ref, wu_ref, wd_ref, o_ref,
                u2d, slots, x_t, o_t, rt_i, gates, cnt_v, rt_si, cnt_s, lst_src, lst_dst,
                *, blk, tm, pitch, cap):
    e = pl.program_id(1)
    n_chunks = blk // LANES
    d = h_ref.shape[1]
    n_slab = d // LANES

    @pl.when(e == 0)
    def _router():
        g = g_ref[...]
        wr_hi, wr_lo = _split2(wr_ref[...])
        sub16 = lax.broadcasted_iota(I32, (N_EXPERTS, LANES), 0)
        tri = (lax.broadcasted_iota(I32, (LANES, LANES), 0)
               < lax.broadcasted_iota(I32, (LANES, LANES), 1)).astype(BF16)

        def chunk(c, carry):
            r0 = pl.multiple_of(c * LANES, LANES)
            u = _rms(h_ref[pl.ds(r0, LANES), :], g)
            for j in range(n_slab):
                u2d[pl.ds(c * (LANES * n_slab) + j, LANES, stride=n_slab), :] = u[:, j * LANES:(j + 1) * LANES]
            u_hi, u_lo = _split2(u)
            lt = _dot_nt(wr_hi, u_hi) + (_dot_nt(wr_hi, u_lo) + _dot_nt(wr_lo, u_hi))
            gl = [lt[k:k + 1, :] for k in range(N_GROUPS)]
            gmax = jnp.maximum(jnp.maximum(gl[0], gl[1]), jnp.maximum(gl[2], gl[3]))
            gsel = jnp.where(gl[0] >= gmax, 0, jnp.where(gl[1] >= gmax, 1, jnp.where(gl[2] >= gmax, 2, 3)))
            denom = (jnp.exp(gl[0] - gmax) + jnp.exp(gl[1] - gmax)) + (jnp.exp(gl[2] - gmax) + jnp.exp(gl[3] - gmax))
            p_top = 1.0 / denom
            le = []
            for k in range(EXPERTS_PER_GROUP):
                rows = [lt[N_GROUPS + gi * EXPERTS_PER_GROUP + k:N_GROUPS + gi * EXPERTS_PER_GROUP + k + 1, :]
                        for gi in range(N_GROUPS)]
                le.append(jnp.where(gsel == 0, rows[0], jnp.where(gsel == 1, rows[1],
                                                                   jnp.where(gsel == 2, rows[2], rows[3]))))
            v1 = jnp.maximum(jnp.maximum(le[0], le[1]), jnp.maximum(le[2], le[3]))
            i1 = jnp.where(le[0] >= v1, 0, jnp.where(le[1] >= v1, 1, jnp.where(le[2] >= v1, 2, 3)))
            rest = [jnp.where(i1 == k, NEG_INF, le[k]) for k in range(EXPERTS_PER_GROUP)]
            v2 = jnp.maximum(jnp.maximum(rest[0], rest[1]), jnp.maximum(rest[2], rest[3]))
            i2 = jnp.where((rest[0] >= v2) & (i1 != 0), 0,
                           jnp.where((rest[1] >= v2) & (i1 != 1), 1,
                                     jnp.where((rest[2] >= v2) & (i1 != 2), 2, 3)))
            ex = jnp.exp(v2 - v1)
            gate1 = p_top / (1.0 + ex)
            gate2 = p_top * ex / (1.0 + ex)
            e1 = gsel * EXPERTS_PER_GROUP + i1
            e2 = gsel * EXPERTS_PER_GROUP + i2
            oh1 = (sub16 == e1).astype(F32)
            oh2 = (sub16 == e2).astype(F32)
            both = oh1 + oh2
            rank = _dot(both.astype(BF16), tri) + carry
            pos1 = jnp.sum(oh1 * rank, axis=0, keepdims=True).astype(I32)
            pos2 = jnp.sum(oh2 * rank, axis=0, keepdims=True).astype(I32)
            rt_i[c] = jnp.zeros((SUBLANES, LANES), I32)
            gates[c] = jnp.zeros((SUBLANES, LANES), F32)
            rt_i[c, 0:1, :] = e1 * cap + pos1
            rt_i[c, 1:2, :] = e2 * cap + pos2
            gates[c, 0:1, :] = gate1
            gates[c, 1:2, :] = gate2
            return carry + jnp.sum(both, axis=1, keepdims=True)

        total = lax.fori_loop(0, n_chunks, chunk, jnp.zeros((N_EXPERTS, 1), F32))
        cnt_v[...] = jnp.broadcast_to(total, (N_EXPERTS, LANES)).astype(I32)
        pltpu.sync_copy(rt_i, rt_si)
        pltpu.sync_copy(cnt_v, cnt_s)

        def invert(c, _):
            for l in range(LANES):
                row = (c * LANES + l) * n_slab
                a1 = rt_si[c, 0, l]
                a2 = rt_si[c, 1, l]
                lst_src[a1] = row
                lst_src[a2] = row
                lst_dst[a1] = row
                lst_dst[a2] = row + blk * n_slab
            return 0

        lax.fori_loop(0, n_chunks, invert, 0)

        for ex in range(N_EXPERTS):
            n = cnt_s[ex, 0]
            n_pad = ((n + (tm - 1)) // tm) * tm

            def pad(i, _, ex=ex, n=n):
                lst_src[ex * cap + i] = lst_src[ex * cap + n - 1]
                lst_dst[ex * cap + i] = lst_dst[ex * cap + n - 1]
                return 0

            lax.fori_loop(n, n_pad, pad, 0)

    n_e = cnt_s[e, 0]
    n_tiles = (n_e + (tm - 1)) // tm

    def tile(ti, _):
        base = e * cap + ti * tm
        for r in range(tm):
            src = pl.multiple_of(lst_src[base + r], n_slab)
            x_t[pl.ds(r, n_slab, stride=pitch), :] = u2d[pl.ds(src, n_slab), :]
        x = jnp.concatenate([x_t[j * pitch:j * pitch + tm, :] for j in range(n_slab)], axis=1).astype(BF16)
        hg = _dot(x, wg_ref[0, 0])
        hu = _dot(x, wu_ref[0, 0])
        act = (hg * jax.nn.sigmoid(hg) * hu).astype(BF16)
        o = _dot(act, wd_ref[0, 0])
        for j in range(n_slab):
            o_t[j * pitch:j * pitch + tm, :] = o[:, j * LANES:(j + 1) * LANES]
        for r in range(tm):
            dst = pl.multiple_of(lst_dst[base + r], n_slab)
            slots[pl.ds(dst, n_slab), :] = o_t[pl.ds(r, n_slab, stride=pitch), :]
        return 0

    lax.fori_loop(0, n_tiles, tile, 0)

    @pl.when(e == N_EXPERTS - 1)
    def _combine():
        eye = (lax.broadcasted_iota(I32, (LANES, LANES), 0)
               == lax.broadcasted_iota(I32, (LANES, LANES), 1)).astype(BF16)

        def chunk(c, _):
            r0 = pl.multiple_of(c * LANES, LANES)
            g0, g1, g2 = _split3(gates[c])
            gcol = _dot_nt(eye, g0) + (_dot_nt(eye, g1) + _dot_nt(eye, g2))
            ga, gb = gcol[:, 0:1], gcol[:, 1:2]
            for j in range(n_slab):
                a = slots[pl.ds(c * (LANES * n_slab) + j, LANES, stride=n_slab), :]
                b = slots[pl.ds((blk + c * LANES) * n_slab + j, LANES, stride=n_slab), :]
                o_ref[pl.ds(r0, LANES), j * LANES:(j + 1) * LANES] = (
                    h_ref[pl.ds(r0, LANES), j * LANES:(j + 1) * LANES] + (ga * a + gb * b))
            return 0

        lax.fori_loop(0, n_chunks, chunk, 0)


def _moe(h, g, wr_t, wg, wu, wd, *, layer, blk, tm):
    t_rows, d = h.shape
    f = wg.shape[3]
    n_blocks = t_rows // blk
    n_chunks = blk // LANES
    n_slab = d // LANES
    pitch = tm + SUBLANES
    cap = -(-blk // tm) * tm
    kern = functools.partial(_moe_kernel, blk=blk, tm=tm, pitch=pitch, cap=cap)
    return pl.pallas_call(
        kern,
        grid=(n_blocks, N_EXPERTS),
        in_specs=[
            pl.BlockSpec((blk, d), lambda i, e: (i, 0)),
            pl.BlockSpec((1, d), lambda i, e: (0, 0)),
            pl.BlockSpec(wr_t.shape, lambda i, e: (0, 0)),
            pl.BlockSpec((1, 1, d, f), lambda i, e: (layer, e, 0, 0)),
            pl.BlockSpec((1, 1, d, f), lambda i, e: (layer, e, 0, 0)),
            pl.BlockSpec((1, 1, f, d), lambda i, e: (layer, e, 0, 0)),
        ],
        out_specs=pl.BlockSpec((blk, d), lambda i, e: (i, 0), pipeline_mode=pl.Buffered(1)),
        out_shape=jax.ShapeDtypeStruct(h.shape, F32),
        scratch_shapes=[
            pltpu.VMEM((blk * n_slab, LANES), F32),
            pltpu.VMEM((2 * blk * n_slab, LANES), F32),
            pltpu.VMEM((n_slab * pitch, LANES), F32),
            pltpu.VMEM((n_slab * pitch, LANES), F32),
            pltpu.VMEM((n_chunks, SUBLANES, LANES), I32),
            pltpu.VMEM((n_chunks, SUBLANES, LANES), F32),
            pltpu.VMEM((N_EXPERTS, LANES), I32),
            pltpu.SMEM((n_chunks, SUBLANES, LANES), I32),
            pltpu.SMEM((N_EXPERTS, LANES), I32),
            pltpu.SMEM((N_EXPERTS * cap,), I32),
            pltpu.SMEM((N_EXPERTS * cap,), I32),
        ],
        input_output_aliases={0: 0},
        compiler_params=_params(("arbitrary", "arbitrary"), VMEM_LIMIT_BYTES),
        name="moe",
    )(h, g, wr_t, wg, wu, wd)


def _kv_rows(x, g_ref, wkv_ref, wf_ref, bf_ref, kg_ref, gg_ref):
    d = x.shape[1]
    u = _rms(x, g_ref[...])
    p = _dot(u.astype(BF16), wkv_ref[...])
    kn = _head_norm(p[:, :d], gg_ref[...], kg_ref[...])
    z = _dot3(u, wf_ref[...])[:, :N_HEADS] + bf_ref[...]
    return kn, p[:, d:], jnp.minimum(z, 0.0) - jnp.log1p(jnp.exp(-jnp.abs(z)))


def _kv_prompt_kernel(h_ref, hm_ref, g_ref, wkv_ref, wf_ref, bf_ref, kg_ref, gg_ref,
                      kp_ref, vp_ref, lp_ref, kb_ref, vb_ref, kbt_ref, vbt_ref, lfm_ref, lft_ref, *, tt):
    i = pl.program_id(1)
    w = (g_ref, wkv_ref, wf_ref, bf_ref, kg_ref, gg_ref)

    @pl.when(i == 0)
    def _():
        kn, v, lf = _kv_rows(hm_ref[...], *w)
        kp_ref[0, 0:N_META, :] = kn
        vp_ref[0, 0:N_META, :] = v
        lp_ref[0, 0:N_META, :] = lf
        kbt_ref[...] = kn.astype(BF16)
        vbt_ref[...] = v.astype(BF16)
        lft_ref[...] = lf

    kn, v, lf = _kv_rows(h_ref[...], *w)
    r0 = pl.multiple_of(N_META + i * tt, SUBLANES)
    kp_ref[0, pl.ds(r0, tt), :] = kn
    vp_ref[0, pl.ds(r0, tt), :] = v
    lp_ref[0, pl.ds(r0, tt), :] = lf
    kb_ref[...] = kn.astype(BF16)
    vb_ref[...] = v.astype(BF16)
    lfm_ref[...] = lf


def _kv_prompt(h, h_meta, g, wkv, wf, bfr, kg, gg, *, n_batch, seq, tt):
    d = h.shape[1]
    n_t = seq // tt
    length = seq + N_META
    const = lambda b, i: (0, 0)
    tile = lambda b, i: (b * n_t + i, 0)
    per_b = lambda b, i: (b, 0)
    whole = lambda b, i: (b, 0, 0)
    return pl.pallas_call(
        functools.partial(_kv_prompt_kernel, tt=tt),
        grid=(n_batch, n_t),
        in_specs=[pl.BlockSpec((tt, d), tile), pl.BlockSpec((N_META, d), per_b),
                  pl.BlockSpec((1, d), const), pl.BlockSpec(wkv.shape, const), pl.BlockSpec(wf.shape, const),
                  pl.BlockSpec((1, N_HEADS), const), pl.BlockSpec((1, d), const), pl.BlockSpec(gg.shape, const)],
        out_specs=[pl.BlockSpec((1, length, d), whole), pl.BlockSpec((1, length, d), whole),
                   pl.BlockSpec((1, length, N_HEADS), whole),
                   pl.BlockSpec((tt, d), tile), pl.BlockSpec((tt, d), tile),
                   pl.BlockSpec((N_META, d), per_b), pl.BlockSpec((N_META, d), per_b),
                   pl.BlockSpec((tt, N_HEADS), tile), pl.BlockSpec((N_META, N_HEADS), per_b)],
        out_shape=[jax.ShapeDtypeStruct((n_batch, length, d), F32), jax.ShapeDtypeStruct((n_batch, length, d), F32),
                   jax.ShapeDtypeStruct((n_batch, length, N_HEADS), F32),
                   jax.ShapeDtypeStruct((n_batch * seq, d), BF16), jax.ShapeDtypeStruct((n_batch * seq, d), BF16),
                   jax.ShapeDtypeStruct((n_batch * N_META, d), BF16), jax.ShapeDtypeStruct((n_batch * N_META, d), BF16),
                   jax.ShapeDtypeStruct((n_batch * seq, N_HEADS), F32),
                   jax.ShapeDtypeStruct((n_batch * N_META, N_HEADS), F32)],
        compiler_params=_params(("arbitrary", "arbitrary"), VMEM_LIMIT_BYTES),
        name="kv_prompt",
    )(h, h_meta, g, wkv, wf, bfr, kg, gg)


def _kv_sample_kernel(h_ref, g_ref, wkv_ref, wf_ref, bf_ref, kg_ref, gg_ref, k_ref, v_ref, lf_ref):
    k_ref[...], v_ref[...], lf_ref[...] = _kv_rows(h_ref[...], g_ref, wkv_ref, wf_ref, bf_ref, kg_ref, gg_ref)


def _kv_sample(h_s, g, wkv, wf, bfr, kg, gg):
    rows, d = h_s.shape
    return pl.pallas_call(
        _kv_sample_kernel,
        out_shape=[jax.ShapeDtypeStruct((rows, d), F32), jax.ShapeDtypeStruct((rows, d), F32),
                   jax.ShapeDtypeStruct((rows, N_HEADS), F32)],
        compiler_params=_params(None, VMEM_LIMIT_BYTES),
        name="kv_sample",
    )(h_s, g, wkv, wf, bfr, kg, gg)


def _q_kernel(h_ref, g_ref, wq_ref, qg_ref, gg_ref, q_ref):
    u = _rms(h_ref[...], g_ref[...])
    q = _dot(u.astype(BF16), wq_ref[...])
    q_ref[...] = (_head_norm(q, gg_ref[...], qg_ref[...]) * ATTN_SCALE).astype(BF16)


def _q_proj(h, g, wq, qg, gg, *, tt):
    t_rows, d = h.shape
    row = lambda i: (i, 0)
    const = lambda i: (0, 0)
    return pl.pallas_call(
        _q_kernel,
        grid=(t_rows // tt,),
        in_specs=[pl.BlockSpec((tt, d), row), pl.BlockSpec((1, d), const), pl.BlockSpec(wq.shape, const),
                  pl.BlockSpec((1, d), const), pl.BlockSpec(gg.shape, const)],
        out_specs=pl.BlockSpec((tt, d), row),
        out_shape=jax.ShapeDtypeStruct((t_rows, d), BF16),
        compiler_params=_params(("arbitrary",), VMEM_LIMIT_BYTES),
        name="q_proj",
    )(h, g, wq, qg, gg)


def _o_kernel(h_ref, am_ref, at_ref, wo_ref, o_ref, *, n_main):
    a = jnp.where(pl.program_id(0) < n_main, am_ref[...], at_ref[...])
    o_ref[...] = h_ref[...] + _dot(a, wo_ref[...])


def _o_proj(h, a_main, a_tail, wo, *, tt):
    t_rows, d = h.shape
    n_main = a_main.shape[0] // tt
    row = lambda i: (i, 0)
    return pl.pallas_call(
        functools.partial(_o_kernel, n_main=n_main),
        grid=(t_rows // tt,),
        in_specs=[pl.BlockSpec((tt, d), row),
                  pl.BlockSpec((tt, d), lambda i: (jnp.minimum(i, n_main - 1), 0)),
                  pl.BlockSpec((tt, d), lambda i: (jnp.maximum(i - n_main, 0), 0)),
                  pl.BlockSpec(wo.shape, lambda i: (0, 0))],
        out_specs=pl.BlockSpec((tt, d), row),
        out_shape=jax.ShapeDtypeStruct(h.shape, F32),
        input_output_aliases={0: 0},
        compiler_params=_params(("arbitrary",), VMEM_LIMIT_BYTES),
        name="o_proj",
    )(h, a_main, a_tail, wo)


def _bias_placement(d):
    h = jnp.arange(N_HEADS)
    base = (h // 2) * LANES + 6 * (h % 2)
    col = jnp.arange(d)[None, None, :]
    i = jnp.arange(3)[:, None, None]
    hit_q = col == (base[None, :, None] + i)
    hit_k = col == (base[None, :, None] + i + 3)
    place_q = hit_q.astype(BF16)
    place_k = -hit_k.astype(BF16)
    ones_q = jnp.any(hit_k, axis=(0, 1)).astype(F32)[None]
    ones_k = jnp.any(hit_q, axis=(0, 1)).astype(F32)[None]
    return place_q, place_k, ones_q, ones_k


def _cumsum_kernel(lm_ref, lt_ref, pq_ref, pk_ref, oq_ref, ok_ref, bqm_ref, bkm_ref, bqt_ref, bkt_ref, ct_ref,
                   *, seq, tc):
    def lanes(c):
        pieces = _split3(c)
        bq = _dot(pieces[0], pq_ref[0]) + (_dot(pieces[1], pq_ref[1]) + _dot(pieces[2], pq_ref[2])) + oq_ref[...]
        bk = _dot(pieces[0], pk_ref[0]) + (_dot(pieces[1], pk_ref[1]) + _dot(pieces[2], pk_ref[2])) + ok_ref[...]
        return bq.astype(BF16), bk.astype(BF16)

    cur = jnp.zeros((1, N_HEADS), F32)
    for r in range(N_META):
        cur = cur + lt_ref[r:r + 1, :]
        ct_ref[r:r + 1, :] = cur
    bqt_ref[...], bkt_ref[...] = lanes(ct_ref[...])
    tril = (lax.broadcasted_iota(I32, (tc, tc), 1) <= lax.broadcasted_iota(I32, (tc, tc), 0)).astype(BF16)
    for c in range(seq // tc):
        rows = slice(c * tc, (c + 1) * tc)
        blk = _dot_exact_rhs(tril, lm_ref[rows, :]) + cur
        bqm_ref[rows, :], bkm_ref[rows, :] = lanes(blk)
        cur = blk[tc - 1:tc, :]


def _cumsum(lf_main, lf_meta, d, *, n_batch, seq, tc):
    kern = functools.partial(_cumsum_kernel, seq=seq, tc=tc)
    place_q, place_k, ones_q, ones_k = _bias_placement(d)
    const3 = lambda b: (0, 0, 0)
    const2 = lambda b: (0, 0)
    return pl.pallas_call(
        kern,
        grid=(n_batch,),
        in_specs=[pl.BlockSpec((seq, N_HEADS), lambda b: (b, 0)), pl.BlockSpec((N_META, N_HEADS), lambda b: (b, 0)),
                  pl.BlockSpec(place_q.shape, const3), pl.BlockSpec(place_k.shape, const3),
                  pl.BlockSpec(ones_q.shape, const2), pl.BlockSpec(ones_k.shape, const2)],
        out_specs=[pl.BlockSpec((seq, d), lambda b: (b, 0)), pl.BlockSpec((seq, d), lambda b: (b, 0)),
                   pl.BlockSpec((N_META, d), lambda b: (b, 0)), pl.BlockSpec((N_META, d), lambda b: (b, 0))],
        out_shape=[jax.ShapeDtypeStruct((n_batch * seq, d), BF16), jax.ShapeDtypeStruct((n_batch * seq, d), BF16),
                   jax.ShapeDtypeStruct((n_batch * N_META, d), BF16), jax.ShapeDtypeStruct((n_batch * N_META, d), BF16)],
        scratch_shapes=[pltpu.VMEM((N_META, N_HEADS), F32)],
        compiler_params=_params(("arbitrary",), VMEM_LIMIT_BYTES),
        name="logf_cumsum",
    )(lf_main, lf_meta, place_q, place_k, ones_q, ones_k)


def _attend(s, vx, mask, m, acc):
    if mask is not None:
        s = jnp.where(mask, s, NEG_INF)
    m_new = jnp.maximum(m, jnp.max(s, axis=1, keepdims=True))
    alpha = jnp.exp(m - m_new)
    p = jnp.exp(s - m_new).astype(BF16)
    return m_new, alpha * acc + _dot(p, vx)


def _attn_kernel(qm_ref, qt_ref, km_ref, kt_ref, vm_ref, vt_ref, bqm_ref, bqt_ref, bkm_ref, bkt_ref,
                 om_ref, ot_ref, *, seq, tq):
    lane = lax.broadcasted_iota(I32, (1, LANES), 1)
    first = lane < HEAD_DIM
    one = jnp.ones((), BF16)

    def q_heads(q2, bq):
        keep = [lane < 6, (lane >= 6) & (lane < 12)]
        return [jnp.concatenate([jnp.where(first, q2, 0) if hd == 0 else jnp.where(first, 0, q2),
                                 jnp.where(keep[hd], bq, 0)], axis=1) for hd in range(2)]

    def v_heads(v2):
        return [jnp.where(first, v2, one), jnp.where(first, one, v2)]

    def finish(accs):
        outs = [acc / pltpu.roll(acc, HEAD_DIM, axis=1) for acc in accs]
        return jnp.where(first, outs[0], outs[1]).astype(BF16)

    def init(rows):
        return jnp.full((rows, 1), NEG_INF, F32), jnp.zeros((rows, LANES), F32)

    def scores(qa, ka):
        return _dot_nt(qa[0], ka), _dot_nt(qa[1], ka)

    def update(s, vx, mask, state):
        return _attend(s[0], vx[0], mask, state[0], state[1]) + _attend(s[1], vx[1], mask, state[2], state[3])

    def k_rows(c0):
        return jnp.concatenate([km_ref[pl.ds(c0, tq), :], bkm_ref[pl.ds(c0, tq), :]], axis=1)

    ka_t = jnp.concatenate([kt_ref[...], bkt_ref[...]], axis=1)
    vx_t = v_heads(vt_ref[...])

    causal_t = (lax.broadcasted_iota(I32, (N_META, N_META), 1) <= lax.broadcasted_iota(I32, (N_META, N_META), 0))
    st = update(scores(q_heads(qt_ref[...], bqt_ref[...]), ka_t), vx_t, causal_t, init(N_META) + init(N_META))
    ot_ref[...] = finish([st[1], st[3]])

    causal = (lax.broadcasted_iota(I32, (tq, tq), 1) <= lax.broadcasted_iota(I32, (tq, tq), 0))
    for qi in range(seq // tq):
        r0 = qi * tq
        qa = q_heads(qm_ref[r0:r0 + tq, :], bqm_ref[r0:r0 + tq, :])
        state = update(scores(qa, ka_t), vx_t, None, init(tq) + init(tq))

        s = scores(qa, k_rows(0))
        for j in range(qi):
            s_next = scores(qa, k_rows((j + 1) * tq))
            state = update(s, v_heads(vm_ref[j * tq:(j + 1) * tq, :]), None, state)
            s = s_next
        state = update(s, v_heads(vm_ref[r0:r0 + tq, :]), causal, state)
        om_ref[r0:r0 + tq, :] = finish([state[1], state[3]])


def _prompt_attn(q, k, k_meta, v, v_meta, bq_main, bq_meta, bk_main, bk_meta, *, n_batch, seq, tq):
    t_rows, d = q.shape
    n_pairs = d // LANES
    meta0 = (n_batch * seq) // N_META
    main = pl.BlockSpec((seq, LANES), lambda b, p: (b, p))
    meta = pl.BlockSpec((N_META, LANES), lambda b, p: (meta0 + b, p))
    small = pl.BlockSpec((N_META, LANES), lambda b, p: (b, p))
    kern = functools.partial(_attn_kernel, seq=seq, tq=tq)
    return pl.pallas_call(
        kern,
        grid=(n_batch, n_pairs),
        in_specs=[main, meta, main, small, main, small, main, small, main, small],
        out_specs=[main, small],
        out_shape=[jax.ShapeDtypeStruct((n_batch * seq, d), BF16), jax.ShapeDtypeStruct((n_batch * N_META, d), BF16)],
        compiler_params=_params(("arbitrary", "arbitrary"), VMEM_LIMIT_BYTES),
        name="prompt_attn",
    )(q, q, k, k_meta, v, v_meta, bq_main, bq_meta, bk_main, bk_meta)


def _decode_kernel(pt_ref, q_ref, *refs, n_steps, n_new, ppg):
    kc_refs, vc_refs, lc_refs = refs[:ppg], refs[ppg:2 * ppg], refs[2 * ppg:3 * ppg]
    kn_ref, vn_ref, ln_ref, o_ref, qbd, qbd_b, csq, m_s, l_s, acc_s, run_s = refs[3 * ppg:]
    j = pl.program_id(1)
    d = q_ref.shape[2]
    rows = n_new * N_HEADS
    head_of_col = lax.broadcasted_iota(I32, (N_HEADS, d), 1) // HEAD_DIM
    head_of_row = lax.broadcasted_iota(I32, (N_HEADS, d), 0)
    diag = head_of_col == head_of_row

    @pl.when(j == 0)
    def _():
        for qi in range(n_new):
            qbd[qi * N_HEADS:(qi + 1) * N_HEADS, :] = jnp.where(diag, q_ref[0, qi:qi + 1, :].astype(F32), 0.0)
        qbd_b[...] = qbd[...].astype(BF16)
        cur = jnp.zeros((N_HEADS, 1), F32)
        for qi in range(n_new):
            cur = cur + ln_ref[0, :, qi:qi + 1]
            csq[qi * N_HEADS:(qi + 1) * N_HEADS, :] = cur
        m_s[...] = jnp.full(m_s.shape, NEG_INF, F32)
        l_s[...] = jnp.zeros(l_s.shape, F32)
        acc_s[...] = jnp.zeros(acc_s.shape, F32)
        run_s[...] = jnp.zeros(run_s.shape, F32)

    page = lc_refs[0].shape[2]
    later = (lax.broadcasted_iota(I32, (page, page), 0) > lax.broadcasted_iota(I32, (page, page), 1)).astype(BF16)
    qb = qbd_b[...]
    run = run_s[...]
    scores = []
    for pg in range(ppg):
        lf = lc_refs[pg][0]
        l0, l1, l2 = _split3(lf)
        suffix = _dot(l0, later) + (_dot(l1, later) + _dot(l2, later)) + run
        run = run + jnp.sum(lf, axis=1, keepdims=True)
        scores.append(_dot(qb, kc_refs[pg][0].astype(BF16)) + jnp.concatenate([suffix] * n_new, axis=0))
    run_s[...] = run
    s = jnp.concatenate(scores, axis=1) + csq[...]
    m_new = jnp.maximum(m_s[...], jnp.max(s, axis=1, keepdims=True))
    alpha = jnp.exp(m_s[...] - m_new)
    p32 = jnp.exp(s - m_new)
    l_s[...] = alpha * l_s[...] + jnp.sum(p32, axis=1, keepdims=True)
    p = p32.astype(BF16)
    pv =_dot_nt(p[:, 0:page], vc_refs[0][0].astype(BF16))
    for pg in range(1, ppg):
        pv = pv + _dot_nt(p[:, pg * page:(pg + 1) * page], vc_refs[pg][0].astype(BF16))
    acc_s[...] = alpha * acc_s[...] + pv
    m_s[...] = m_new

    @pl.when(j == n_steps - 1)
    def _():
        row_q = lax.broadcasted_iota(I32, (rows, 1), 0) // N_HEADS
        m, l, acc = m_s[...], l_s[...], acc_s[...]
        qf = qbd[...]
        for jn in range(n_new):
            kn = kn_ref[0, jn:jn + 1, :]
            s = jnp.sum(qf * kn, axis=1, keepdims=True)
            cj = csq[jn * N_HEADS:(jn + 1) * N_HEADS, :]
            s = s + (csq[...] - jnp.concatenate([cj] * n_new, axis=0))
            s = jnp.where(row_q >= jn, s, NEG_INF)
            m_new = jnp.maximum(m, s)
            alpha = jnp.exp(m - m_new)
            p = jnp.exp(s - m_new)
            l = alpha * l + p
            acc = alpha * acc + p * vn_ref[0, jn:jn + 1, :]
            m = m_new
        out = acc / l
        for qi in range(n_new):
            blk = jnp.where(diag, out[qi * N_HEADS:(qi + 1) * N_HEADS, :], 0.0)
            o_ref[0, qi:qi + 1, :] = jnp.sum(blk, axis=0, keepdims=True).astype(o_ref.dtype)


def _decode_attn(page_table, q, kc, vc, lc, kn, vn, ln_t):
    n_seq, n_new, d = q.shape
    n_pages = page_table.shape[1]
    page = kc.shape[2]
    rows = n_new * N_HEADS
    ppg = _pick((n_pages,), (16, 8, 4, 2, 1))
    n_steps = n_pages // ppg
    kern = functools.partial(_decode_kernel, n_steps=n_steps, n_new=n_new, ppg=ppg)
    seq3 = lambda n, j, pt: (n, 0, 0)

    def pg3(pg):
        return lambda n, j, pt: (pt[n, n_pages - 1 - (j * ppg + pg)], 0, 0)

    grid_spec = pltpu.PrefetchScalarGridSpec(
        num_scalar_prefetch=1,
        grid=(n_seq, n_steps),
        in_specs=([pl.BlockSpec((1, n_new, d), seq3)]
                  + [pl.BlockSpec((1, d, page), pg3(pg)) for pg in range(ppg)]
                  + [pl.BlockSpec((1, d, page), pg3(pg)) for pg in range(ppg)]
                  + [pl.BlockSpec((1, N_HEADS, page), pg3(pg)) for pg in range(ppg)]
                  + [pl.BlockSpec((1, n_new, d), seq3), pl.BlockSpec((1, n_new, d), seq3),
                     pl.BlockSpec((1, N_HEADS, n_new), seq3)]),
        out_specs=pl.BlockSpec((1, n_new, d), seq3),
        scratch_shapes=[pltpu.VMEM((rows, d), F32), pltpu.VMEM((rows, d), BF16), pltpu.VMEM((rows, 1), F32),
                        pltpu.VMEM((rows, 1), F32), pltpu.VMEM((rows, 1), F32), pltpu.VMEM((rows, d), F32),
                        pltpu.VMEM((N_HEADS, 1), F32)],
    )
    return pl.pallas_call(
        kern,
        grid_spec=grid_spec,
        out_shape=jax.ShapeDtypeStruct((n_seq, n_new, d), F32),
        compiler_params=_params(("arbitrary", "arbitrary"), VMEM_LIMIT_BYTES),
        name="decode_attn",
    )(page_table, q, *([kc] * ppg), *([vc] * ppg), *([lc] * ppg), kn, vn, ln_t)


def _pick(totals, cands):
    for c in cands:
        if all(t % c == 0 for t in totals):
            return c
    raise ValueError(f"no tile among {cands} divides {totals}")


def kernel(x_prompt, x_sample, state_pool, cache_k, cache_v, cache_logf, page_table, meta, pool_norm_g, pool_w,
           pool_scale, kv_norm_g, w_kvf, b_f, k_norm_g, attn_norm_g, w_q, q_norm_g, w_o, ffn_norm_g,
           w_router_group, w_router_expert, w_gate, w_up, w_down):
    n_batch, seq, d = x_prompt.shape
    n_seq, n_new, _ = x_sample.shape
    depth = ffn_norm_g.shape[0]
    n_a = pool_norm_g.shape[0]
    n_phys, page = cache_k.shape[0], cache_k.shape[1]
    past_len = page_table.shape[1] * page
    r_main, r_meta, r_s = n_batch * seq, n_batch * N_META, n_seq * n_new
    t_rows = r_main + r_meta + r_s
    assert d == N_HEADS * HEAD_DIM and d % (LANES * SUBLANES) == 0 and n_batch % SUBLANES == 0 and n_seq % SUBLANES == 0

    tt_pool = _pick((seq, t_rows), (512, 256, 128))
    tt_proj = _pick((t_rows,), (640, 512, 256, 128))
    tt_o = _pick((r_main, r_meta + r_s), (256, 128))
    blk_moe = _pick((t_rows,), (1664, 1280, 1024, 512, 256, 128))
    tq = _pick((seq,), (512, 256, 128))
    tc = _pick((seq,), (256, 128))

    xs_t = jnp.transpose(x_sample, (1, 0, 2))
    h = jnp.concatenate([x_prompt.reshape(r_main, d),
                         jnp.broadcast_to(meta[None], (n_batch, N_META, d)).reshape(r_meta, d),
                         xs_t.reshape(r_s, d)], axis=0)

    gg = (jnp.arange(MXU_DIM)[:, None] // HEAD_DIM == jnp.arange(MXU_DIM)[None, :] // HEAD_DIM).astype(BF16)
    state_t = jnp.transpose(state_pool, (0, 2, 1, 3))
    zero_prev = jnp.zeros((POOL_STATE, n_batch, d), F32)

    def put(hbuf, rows, start):
        return lax.dynamic_update_slice(hbuf, rows, (start, 0))

    def moe(hbuf, layer):
        wr = jnp.concatenate([w_router_group[layer].T,
                              jnp.transpose(w_router_expert[layer], (0, 2, 1)).reshape(N_EXPERTS, d),
                              jnp.zeros((32 - N_GROUPS - N_EXPERTS, d), F32)], axis=0)
        return _moe(hbuf, ffn_norm_g[layer][None], wr, wg_b, wu_b, wd_b, layer=layer, blk=blk_moe, tm=128)

    wg_b, wu_b, wd_b = w_gate.astype(BF16), w_up.astype(BF16), w_down.astype(BF16)

    tails_p, tails_s = [], []
    for layer in range(n_a):
        g = pool_norm_g[layer][None]
        sc = pool_scale[layer][None]
        w = pool_w[layer]
        h_meta = h[r_main:r_main + r_meta]
        h_s = h[r_main + r_meta:]
        h, tail = _pool_main(h, h_meta, g, w, sc, n_batch=n_batch, seq=seq, tt=tt_pool)
        meta_t = jnp.transpose(h_meta.reshape(n_batch, N_META, d), (1, 0, 2))
        y_meta, _ = _pool_small(meta_t, zero_prev, g, w, sc, pos0=0)
        y_s, u_s = _pool_small(h_s.reshape(n_new, n_seq, d), state_t[layer], g, w, sc, pos0=past_len)
        h = put(h, jnp.transpose(y_meta, (1, 0, 2)).reshape(r_meta, d), r_main)
        h = put(h, y_s.reshape(r_s, d), r_main + r_meta)
        tails_p.append(jnp.transpose(tail[:, 1:], (1, 0, 2)))
        tails_s.append(jnp.concatenate([state_t[layer][n_new:], u_s], axis=0))
        h = moe(h, layer)

    wkv = w_kvf[:, :2 * d].astype(BF16)
    wf = jnp.pad(w_kvf[:, 2 * d:], ((0, 0), (0, LANES - N_HEADS)))
    s0 = r_main + r_meta
    kv_w = (kv_norm_g[None], wkv, wf, b_f[None], jnp.tile(k_norm_g, N_HEADS)[None], gg)
    k_p, v_p, logf_p, kb, vb, kb_meta, vb_meta, lf_main, lf_meta = _kv_prompt(
        h, h[r_main:s0], *kv_w, n_batch=n_batch, seq=seq, tt=tt_pool)
    k_sr, v_sr, lf_sr = _kv_sample(h[s0:], *kv_w)
    bq_main, bk_main, bq_meta, bk_meta = _cumsum(lf_main, lf_meta, d, n_batch=n_batch, seq=seq, tc=tc)

    kc = jnp.transpose(cache_k, (0, 2, 3, 1)).reshape(n_phys, d, page)
    vc = jnp.transpose(cache_v, (0, 2, 3, 1)).reshape(n_phys, d, page)
    lc = jnp.transpose(cache_logf, (0, 2, 1))

    def to_seq_major(rows):
        return jnp.transpose(rows.reshape(n_new, n_seq, rows.shape[-1]), (1, 0, 2))

    k_s, v_s, lf_s = to_seq_major(k_sr), to_seq_major(v_sr), to_seq_major(lf_sr)
    lf_s_t = jnp.transpose(lf_s, (0, 2, 1))

    for layer in range(n_a, depth):
        jb = layer - n_a
        q = _q_proj(h, attn_norm_g[jb][None], w_q[jb].astype(BF16), jnp.tile(q_norm_g[jb], N_HEADS)[None], gg, tt=tt_proj)
        a_main, a_meta = _prompt_attn(q, kb, kb_meta, vb, vb_meta, bq_main, bq_meta, bk_main, bk_meta,
                                      n_batch=n_batch, seq=seq, tq=tq)
        a_s = _decode_attn(page_table, to_seq_major(q[s0:]).astype(F32), kc, vc, lc, k_s, v_s, lf_s_t)
        a_tail = jnp.concatenate([a_meta, jnp.transpose(a_s, (1, 0, 2)).reshape(r_s, d).astype(BF16)], axis=0)
        h = _o_proj(h, a_main, a_tail, w_o[jb].astype(BF16), tt=tt_o)
        h = moe(h, layer)

    y_prompt = h[:r_main].reshape(n_batch, seq, d)
    y_sample = to_seq_major(h[s0:])
    pool_state_prompt = jnp.transpose(jnp.stack(tails_p, axis=0), (0, 2, 1, 3))
    pool_state_sample = jnp.transpose(jnp.stack(tails_s, axis=0), (0, 2, 1, 3))
    k_p = k_p.reshape(n_batch, seq + N_META, N_HEADS, HEAD_DIM)
    v_p = v_p.reshape(n_batch, seq + N_META, N_HEADS, HEAD_DIM)
    return (y_prompt, y_sample, pool_state_prompt, pool_state_sample, k_p, v_p, logf_p,
            k_s.reshape(n_seq, n_new, N_HEADS, HEAD_DIM), v_s.reshape(n_seq, n_new, N_HEADS, HEAD_DIM), lf_s)
```
